```python
import jax, jax.numpy as jnp
from jax import lax
import numpy as np

D_MODEL = 1024
BATCH = 16
SEQ = 2048
DEPTH = 4

BRANCH_W = 512
N_BRANCH = 3
CONV_K = 3
POOL_WINDOWS = (2, 4, 8, 16)
N_POOL_GROUPS = 4
POOL_GROUP_W = BRANCH_W // N_POOL_GROUPS
N_HEADS = 8
N_KV_HEADS = 2
HEAD_DIM = 64
KV_W = N_KV_HEADS * HEAD_DIM
WINDOW = 128
BLOCK = 128
ROPE_THETA = 10000.0
NEG_INF = -1e30
D_FF = 3584
N_EXPERTS = 8
TOP_K = 2
D_FF_EXPERT = 1792
PLE_DIM = 256
EPS = 1e-6

OFF_CONV_C = BRANCH_W
OFF_CONV_U = 2 * BRANCH_W
OFF_POOL = 3 * BRANCH_W
OFF_Q = OFF_POOL + BRANCH_W
OFF_K = OFF_Q + N_HEADS * HEAD_DIM
OFF_V = OFF_K + KV_W
OFF_GATE = OFF_V + KV_W
D_IN = OFF_GATE + N_BRANCH * D_MODEL

kernel_name = 'hybrid_conv_pool_swa_moe_trunk'


def rms_norm(x, g):
    xf = x.astype(jnp.float32)
    y = xf * lax.rsqrt(jnp.mean(xf * xf, axis=-1, keepdims=True) + EPS)
    return (y * g.astype(jnp.float32)).astype(x.dtype)


def rotary(x, positions):
    half = HEAD_DIM // 2
    inv_freq = ROPE_THETA ** (-2.0 * jnp.arange(half, dtype=jnp.float32) / HEAD_DIM)
    ang = positions.astype(jnp.float32)[..., None] * inv_freq
    cos = jnp.cos(ang)[:, :, None, :]
    sin = jnp.sin(ang)[:, :, None, :]
    xf = x.astype(jnp.float32)
    x1, x2 = xf[..., :half], xf[..., half:]
    return jnp.concatenate([x1 * cos - x2 * sin, x2 * cos + x1 * sin], axis=-1).astype(x.dtype)


def short_conv_mixer(b_gate, c_gate, u, conv_w):
    v = c_gate * u
    y = lax.conv_general_dilated(
        v, conv_w.astype(v.dtype)[:, None, :], window_strides=(1,),
        padding=((CONV_K - 1, 0),), dimension_numbers=('NWC', 'WIO', 'NWC'),
        feature_group_count=BRANCH_W)
    return b_gate * y


def pool_mixer(u, pool_w, pool_scale):
    B, S, _ = u.shape
    uf = u.astype(jnp.float32).reshape(B, S, N_POOL_GROUPS, POOL_GROUP_W)
    cs = jnp.cumsum(uf, axis=1)
    cs0 = jnp.pad(cs, ((0, 0), (1, 0), (0, 0), (0, 0)))
    t1 = jnp.arange(1, S + 1)
    lo = jnp.maximum(t1[:, None] - jnp.array(POOL_WINDOWS, dtype=jnp.int32)[None, :], 0)
    g_idx = jnp.arange(N_POOL_GROUPS)[None, :]
    cs_lo = cs0[:, lo, g_idx, :]
    count = (t1[:, None] - lo).astype(jnp.float32)[None, :, :, None]
    pooled = (cs - cs_lo) / count - uf
    mixed = jnp.einsum('bsgc,gcd->bsgd', pooled.astype(u.dtype), pool_w)
    return mixed.reshape(B, S, BRANCH_W) * pool_scale


def sliding_window_attention(q, k, v, sinks):
    B, S = q.shape[0], q.shape[1]
    nb = S // BLOCK
    G = N_HEADS // N_KV_HEADS
    qb = q.reshape(B, nb, BLOCK, N_KV_HEADS, G, HEAD_DIM)

    def band(t):
        prev = jnp.pad(t, ((0, 0), (BLOCK, 0), (0, 0), (0, 0)))[:, :S]
        return jnp.concatenate(
            [prev.reshape(B, nb, BLOCK, N_KV_HEADS, HEAD_DIM),
             t.reshape(B, nb, BLOCK, N_KV_HEADS, HEAD_DIM)], axis=2)

    kb, vb = band(k), band(v)
    scores = jnp.einsum('bnqkgd,bnjkd->bnkgqj', qb, kb,
                        preferred_element_type=jnp.float32) * (HEAD_DIM ** -0.5)
    qi = jnp.arange(BLOCK)[:, None] + BLOCK
    kj = jnp.arange(2 * BLOCK)[None, :]
    dist = qi - kj
    blk = jnp.arange(nb)[:, None, None]
    valid = (dist >= 0) & (dist < WINDOW) & (blk * BLOCK - BLOCK + kj >= 0)
    scores = jnp.where(valid[None, :, None, None], scores, NEG_INF)
    sink = sinks.astype(jnp.float32).reshape(1, 1, N_KV_HEADS, G, 1, 1)
    m = jnp.maximum(jnp.max(scores, axis=-1, keepdims=True), sink)
    pexp = jnp.exp(scores - m)
    denom = jnp.sum(pexp, axis=-1, keepdims=True) + jnp.exp(sink - m)
    probs = (pexp / denom).astype(v.dtype)
    out = jnp.einsum('bnkgqj,bnjkd->bnqkgd', probs, vb)
    return out.reshape(B, S, N_HEADS * HEAD_DIM)


def token_mixing(h, positions, w_in, conv_w, pool_w, pool_scale, sinks, w_branch, w_out):
    B, S, _ = h.shape
    proj = h @ w_in
    a_b, a_c, a_u, pool_u, q, k, v, gate_logits = jnp.split(
        proj, [OFF_CONV_C, OFF_CONV_U, OFF_POOL, OFF_Q, OFF_K, OFF_V, OFF_GATE], axis=-1)
    y_conv = short_conv_mixer(a_b, a_c, a_u, conv_w)
    y_pool = pool_mixer(pool_u, pool_w, pool_scale)
    q = rotary(q.reshape(B, S, N_HEADS, HEAD_DIM), positions)
    k = rotary(k.reshape(B, S, N_KV_HEADS, HEAD_DIM), positions)
    v = v.reshape(B, S, N_KV_HEADS, HEAD_DIM)
    y_att = sliding_window_attention(q, k, v, sinks)
    branches = jnp.stack([y_conv, y_pool, y_att], axis=2)
    branches = jnp.einsum('bsrc,rcd->bsrd', branches, w_branch)
    gates = jax.nn.sigmoid(gate_logits.reshape(B, S, N_BRANCH, D_MODEL))
    merged = jnp.sum(gates * branches, axis=2)
    return merged @ w_out


def swiglu(h, w_gu, w_down, d_ff):
    gu = h @ w_gu
    return (jax.nn.silu(gu[..., :d_ff]) * gu[..., d_ff:]) @ w_down


def moe_ffn(h, w_router, w_gu, w_down):
    B, S, D = h.shape
    ht = h.reshape(B * S, D)
    logits = (ht @ w_router).astype(jnp.float32)
    top_vals, top_idx = lax.top_k(logits, TOP_K)
    top_w = jax.nn.softmax(top_vals, axis=-1)
    combine = jnp.sum(jax.nn.one_hot(top_idx, N_EXPERTS, dtype=jnp.float32) * top_w[..., None], axis=1)
    out = jnp.zeros((B * S, D), jnp.float32)
    for e in range(N_EXPERTS):
        y = swiglu(ht, w_gu[e], w_down[e], D_FF_EXPERT)
        out = out + combine[:, e:e + 1] * y.astype(jnp.float32)
    return out.astype(h.dtype).reshape(B, S, D)


def setup_inputs(seed: int = 0) -> dict:
    key = jax.random.key(seed)
    ks = jax.random.split(key, 24)

    def nrm(k, shape, scale):
        return jax.random.normal(k, shape, jnp.float32) * scale

    def gain(k, shape):
        return 1.0 + 0.01 * jax.random.normal(k, shape, jnp.float32)

    n_dense = (DEPTH + 1) // 2
    n_moe = DEPTH // 2
    return {
        'x': nrm(ks[0], (BATCH, SEQ, D_MODEL), 1.0),
        'p': nrm(ks[1], (DEPTH, BATCH, SEQ, PLE_DIM), 1.0),
        'positions': jnp.broadcast_to(jnp.arange(SEQ, dtype=jnp.int32), (BATCH, SEQ)),
        'norm_mix_g': gain(ks[2], (DEPTH, D_MODEL)),
        'w_in': nrm(ks[3], (DEPTH, D_MODEL, D_IN), D_MODEL ** -0.5),
        'conv_w': nrm(ks[4], (DEPTH, CONV_K, BRANCH_W), CONV_K ** -0.5),
        'pool_w': nrm(ks[5], (DEPTH, N_POOL_GROUPS, POOL_GROUP_W, POOL_GROUP_W), POOL_GROUP_W ** -0.5),
        'pool_scale': 1.0 + 0.1 * jax.random.normal(ks[6], (DEPTH, BRANCH_W), jnp.float32),
        'attn_sinks': nrm(ks[7], (DEPTH, N_HEADS), 1.0),
        'w_branch': nrm(ks[8], (DEPTH, N_BRANCH, BRANCH_W, D_MODEL), BRANCH_W ** -0.5),
        'w_out': nrm(ks[9], (DEPTH, D_MODEL, D_MODEL), D_MODEL ** -0.5),
        'norm_ffn_g': gain(ks[10], (DEPTH, D_MODEL)),
        'ffn_w_gu': nrm(ks[11], (n_dense, D_MODEL, 2 * D_FF), D_MODEL ** -0.5),
        'ffn_w_down': nrm(ks[12], (n_dense, D_FF, D_MODEL), D_FF ** -0.5),
        'moe_router': nrm(ks[13], (n_moe, D_MODEL, N_EXPERTS), D_MODEL ** -0.5),
        'moe_w_gu': nrm(ks[14], (n_moe, N_EXPERTS, D_MODEL, 2 * D_FF_EXPERT), D_MODEL ** -0.5),
        'moe_w_down': nrm(ks[15], (n_moe, N_EXPERTS, D_FF_EXPERT, D_MODEL), D_FF_EXPERT ** -0.5),
        'norm_ple_g': gain(ks[16], (DEPTH, D_MODEL)),
        'ple_w_proj': nrm(ks[17], (DEPTH, PLE_DIM, D_MODEL), PLE_DIM ** -0.5),
        'ple_w_gate': nrm(ks[18], (DEPTH, D_MODEL, D_MODEL), D_MODEL ** -0.5),
        'final_norm_g': gain(ks[19], (D_MODEL,)),
    }


def reference(x, p, positions, norm_mix_g, w_in, conv_w, pool_w, pool_scale, attn_sinks,
              w_branch, w_out, norm_ffn_g, ffn_w_gu, ffn_w_down, moe_router, moe_w_gu,
              moe_w_down, norm_ple_g, ple_w_proj, ple_w_gate, final_norm_g):
    for i in range(DEPTH):
        h = rms_norm(x, norm_mix_g[i])
        x = x + token_mixing(h, positions, w_in[i], conv_w[i], pool_w[i], pool_scale[i],
                             attn_sinks[i], w_branch[i], w_out[i])
        h = rms_norm(x, norm_ffn_g[i])
        if i % 2 == 0:
            x = x + swiglu(h, ffn_w_gu[i // 2], ffn_w_down[i // 2], D_FF)
        else:
            x = x + moe_ffn(h, moe_router[i // 2], moe_w_gu[i // 2], moe_w_down[i // 2])
        h = rms_norm(x, norm_ple_g[i])
        x = x + jax.nn.sigmoid(h @ ple_w_gate[i]) * (p[i] @ ple_w_proj[i])
    return rms_norm(x, final_norm_g)
```

```python
import functools

import jax
import jax.numpy as jnp
from jax import lax
from jax.experimental import pallas as pl
from jax.experimental.pallas import tpu as pltpu

F32 = jnp.float32
BF16 = jnp.bfloat16

BRANCH_W = 512
N_BRANCH = 3
POOL_WINDOWS = (2, 4, 8, 16)
POOL_GROUP_W = 128
N_HEADS = 8
N_KV_HEADS = 2
HEADS_PER_KV = N_HEADS // N_KV_HEADS
HEAD_DIM = 64
HALF = HEAD_DIM // 2
KV_W = N_KV_HEADS * HEAD_DIM
WINDOW = 128
ROPE_THETA = 10000.0
NEG_INF = -1e30
EPS = 1e-6
OFF_POOL = 3 * BRANCH_W
OFF_Q = OFF_POOL + BRANCH_W
OFF_K = OFF_Q + N_HEADS * HEAD_DIM
OFF_GATE = OFF_K + 2 * KV_W

LANES = 128
V7X_VMEM_LIMIT_BYTES = 56 * 1024 * 1024
HALO = 16


def _params(*sem):
    return pltpu.CompilerParams(dimension_semantics=sem, vmem_limit_bytes=V7X_VMEM_LIMIT_BYTES)


def _rms(x, g):
    ms = jnp.mean(x * x, axis=-1, keepdims=True)
    return x * lax.rsqrt(ms + EPS) * g


def _sigmoid(x):
    return 1.0 / (1.0 + jnp.exp(-x))


def _dot(a, b):
    return jnp.dot(a, b, preferred_element_type=F32)


def _rope_kernel(pos_ref, freq_ref, sign_ref, cos_ref, sin_ref):
    ang = pos_ref[...].astype(F32) * freq_ref[...]
    cos_ref[...] = jnp.cos(ang)
    sin_ref[...] = jnp.sin(ang) * sign_ref[...]


def _rope_tables(positions, tm):
    n = positions.size
    inv_freq = ROPE_THETA ** (-2.0 * jnp.arange(HALF, dtype=F32) / HEAD_DIM)
    freq = jnp.tile(inv_freq, LANES // HALF).reshape(1, LANES)
    sign = jnp.tile(jnp.concatenate([-jnp.ones(HALF, F32), jnp.ones(HALF, F32)]), LANES // HEAD_DIM).reshape(1, LANES)
    row = pl.BlockSpec((1, LANES), lambda i: (0, 0))
    tab = pl.BlockSpec((tm, LANES), lambda i: (i, 0))
    return pl.pallas_call(
        _rope_kernel,
        grid=(n // tm,),
        in_specs=[pl.BlockSpec((tm, 1), lambda i: (i, 0)), row, row],
        out_specs=[tab, tab],
        out_shape=[jax.ShapeDtypeStruct((n, LANES), F32)] * 2,
        compiler_params=_params("parallel"),
        name="rope_tables",
    )(positions.reshape(n, 1), freq, sign)


def _rotate_half(x, cos, sin_signed):
    width = x.shape[-1]
    lane = lax.broadcasted_iota(jnp.int32, x.shape, 1)
    first = (lane % HEAD_DIM) < HALF
    partner = jnp.where(first, pltpu.roll(x, width - HALF, 1), pltpu.roll(x, HALF, 1))
    return x * cos + partner * sin_signed


def _in_proj_kernel(x_ref, g_ref, cos_ref, sin_ref, w_ref, conv_ref, pq_ref, kv_ref, gate_ref):
    h = _rms(x_ref[...], g_ref[...]).astype(BF16)

    def proj(c0, width):
        return _dot(h, w_ref[:, c0:c0 + width])

    for c in range(0, OFF_POOL, BRANCH_W):
        conv_ref[:, c:c + BRANCH_W] = proj(c, BRANCH_W).astype(BF16)
    pq_ref[:, :BRANCH_W] = proj(OFF_POOL, BRANCH_W).astype(BF16)
    cos = cos_ref[...]
    sin = sin_ref[...]
    rep = N_HEADS * HEAD_DIM // LANES
    q = _rotate_half(proj(OFF_Q, N_HEADS * HEAD_DIM), jnp.concatenate([cos] * rep, axis=1),
                     jnp.concatenate([sin] * rep, axis=1))
    pq_ref[:, BRANCH_W:] = (q * HEAD_DIM ** -0.5).astype(BF16)
    kv = proj(OFF_K, 2 * KV_W)
    kv_ref[:, :KV_W] = _rotate_half(kv[:, :KV_W], cos, sin).astype(BF16)
    kv_ref[:, KV_W:] = kv[:, KV_W:].astype(BF16)
    d = gate_ref.shape[1] // N_BRANCH
    for r in range(N_BRANCH):
        gate_ref[:, r * d:(r + 1) * d] = proj(OFF_GATE + r * d, d).astype(BF16)


def _in_proj(x, g, cos, sin, w_in, tm):
    n, d = x.shape
    d_in = w_in.shape[1]
    tok = lambda width: pl.BlockSpec((tm, width), lambda i: (i, 0))
    widths = (OFF_POOL, OFF_Q + N_HEADS * HEAD_DIM - OFF_POOL, 2 * KV_W, N_BRANCH * d)
    return pl.pallas_call(
        _in_proj_kernel,
        grid=(n // tm,),
        in_specs=[tok(d), pl.BlockSpec((1, d), lambda i: (0, 0)), tok(LANES), tok(LANES),
                  pl.BlockSpec((d, d_in), lambda i: (0, 0), pipeline_mode=pl.Buffered(1))],
        out_specs=[tok(w) for w in widths],
        out_shape=[jax.ShapeDtypeStruct((n, w), BF16) for w in widths],
        compiler_params=_params("parallel"),
        name="in_proj",
    )(x, g, cos, sin, w_in)


def _block_diag_heads(t, lo, use_rolled):
    rolled = pltpu.roll(t, HEAD_DIM, 1)
    low = jnp.where(lo, rolled if use_rolled else t, 0.0).astype(BF16)
    high = jnp.where(lo, 0.0, t if use_rolled else rolled).astype(BF16)
    z = jnp.zeros_like(low)
    return jnp.concatenate([
        jnp.concatenate([low, z], axis=1), jnp.concatenate([high, z], axis=1),
        jnp.concatenate([z, low], axis=1), jnp.concatenate([z, high], axis=1)], axis=0)


def _mixer_kernel(sinks_ref, conv_ref, convh_ref, pq_ref, poolh_ref, kv_ref, kvh_ref, gate_ref, x_ref,
                  convw_ref, poolw_ref, pscale_ref, wbr_ref, wout_ref, out_ref, att_ref, *, tile):
    seq_tile = pl.program_id(1)
    not_first = seq_tile > 0
    w = BRANCH_W

    v = conv_ref[:, w:2 * w].astype(F32) * conv_ref[:, 2 * w:3 * w].astype(F32)
    vh = convh_ref[:, w:2 * w].astype(F32) * convh_ref[:, 2 * w:3 * w].astype(F32)
    ext = jnp.concatenate([jnp.where(not_first, vh, 0.0), v], axis=0)
    cw = convw_ref[...]
    y = cw[2:3] * v + cw[1:2] * pltpu.roll(ext, 1, 0)[HALO:] + cw[0:1] * pltpu.roll(ext, 2, 0)[HALO:]
    y_conv = (conv_ref[:, :w].astype(F32) * y).astype(BF16)

    pu = pq_ref[:, :w].astype(F32)
    s = jnp.concatenate([jnp.where(not_first, poolh_ref[...].astype(F32), 0.0), pu], axis=0)
    pos1 = seq_tile * tile + lax.broadcasted_iota(jnp.int32, (tile, 1), 0) + 1
    mixed = []
    for gi, win in enumerate(POOL_WINDOWS):
        s = s[:, (POOL_GROUP_W if gi else 0):]
        s = s + pltpu.roll(s, win // 2, 0)
        count = jnp.minimum(pos1, win).astype(F32)
        lo_c = gi * POOL_GROUP_W
        pooled = s[HALO:, :POOL_GROUP_W] / count - pu[:, lo_c:lo_c + POOL_GROUP_W]
        mixed.append(_dot(pooled.astype(BF16), poolw_ref[gi]))
    y_pool = (jnp.concatenate(mixed, axis=1) * pscale_ref[...]).astype(BF16)

    kvh = jnp.where(not_first, kvh_ref[...], jnp.zeros_like(kvh_ref[...]))
    kext = jnp.concatenate([kvh, kv_ref[...]], axis=0)
    lo = lax.broadcasted_iota(jnp.int32, (2 * WINDOW, KV_W), 1) < HEAD_DIM
    row = lax.broadcasted_iota(jnp.int32, (WINDOW, 2 * WINDOW), 0)
    key = lax.broadcasted_iota(jnp.int32, (WINDOW, 2 * WINDOW), 1)
    dist = row + WINDOW - key
    in_band = (dist >= 0) & (dist < WINDOW)
    for j in range(tile // WINDOW):
        band = kext[j * WINDOW:(j + 2) * WINDOW].astype(F32)
        valid = in_band if j else in_band & ((key >= WINDOW) | not_first)
        for g in range(N_KV_HEADS):
            k_bd = _block_diag_heads(band[:, :KV_W], lo, g == 1)
            v_bd = _block_diag_heads(band[:, KV_W:], lo, g == 1)
            qw = HEADS_PER_KV * HEAD_DIM
            qg = pq_ref[j * WINDOW:(j + 1) * WINDOW, w + g * qw:w + (g + 1) * qw]
            scores = lax.dot_general(qg, k_bd, (((1,), (1,)), ((), ())), preferred_element_type=F32)
            probs = []
            for hh in range(HEADS_PER_KV):
                sh = jnp.where(valid, scores[:, hh * 2 * WINDOW:(hh + 1) * 2 * WINDOW], NEG_INF)
                sink = sinks_ref[g * HEADS_PER_KV + hh]
                m = jnp.maximum(jnp.max(sh, axis=-1, keepdims=True), sink)
                pexp = jnp.exp(sh - m)
                denom = jnp.sum(pexp, axis=-1, keepdims=True) + jnp.exp(sink - m)
                probs.append((pexp * (1.0 / denom)).astype(BF16))
            out = _dot(jnp.concatenate(probs, axis=1), v_bd)
            att_ref[j * WINDOW:(j + 1) * WINDOW, g * qw:(g + 1) * qw] = out.astype(BF16)

    d = x_ref.shape[1]
    merged = None
    for r, br in enumerate((y_conv, y_pool, att_ref[...])):
        term = _sigmoid(gate_ref[:, r * d:(r + 1) * d].astype(F32)) * _dot(br, wbr_ref[r])
        merged = term if merged is None else merged + term
    out_ref[...] = x_ref[...] + _dot(merged.astype(BF16), wout_ref[...])


def _mixer(x, conv, pq, kv, gates, sinks, conv_w, pool_w, pool_scale, w_branch, w_out, batch, tile):
    n, d = x.shape
    n_t = n // batch // tile
    t_idx = lambda b, i: b * n_t + i
    tok = lambda width: pl.BlockSpec((tile, width), lambda b, i: (t_idx(b, i), 0))
    prev = lambda rows, width: pl.BlockSpec(
        (rows, width), lambda b, i: (jnp.maximum(t_idx(b, i) * (tile // rows) - 1, 0), 0))
    full = lambda shape: pl.BlockSpec(shape, lambda b, i: (0,) * len(shape))
    return pl.pallas_call(
        functools.partial(_mixer_kernel, tile=tile),
        grid=(batch, n_t),
        in_specs=[pl.BlockSpec(memory_space=pltpu.SMEM),
                  tok(conv.shape[1]), prev(HALO, conv.shape[1]),
                  tok(pq.shape[1]), prev(HALO, BRANCH_W),
                  tok(kv.shape[1]), prev(WINDOW, kv.shape[1]),
                  tok(gates.shape[1]), tok(d),
                  full(conv_w.shape), full(pool_w.shape), full(pool_scale.shape),
                  full(w_branch.shape), full(w_out.shape)],
        out_specs=tok(d),
        out_shape=jax.ShapeDtypeStruct((n, d), F32),
        scratch_shapes=[pltpu.VMEM((tile, N_HEADS * HEAD_DIM), BF16)],
        compiler_params=_params("parallel", "parallel"),
        name="mixer",
    )(sinks, conv, conv, pq, pq, kv, kv, gates, x, conv_w, pool_w, pool_scale, w_branch, w_out)


def _router_kernel(x_ref, g_ref, wr_ref, comb_ref, *, n_experts):
    h = _rms(x_ref[...], g_ref[...]).astype(BF16)
    lane = lax.broadcasted_iota(jnp.int32, comb_ref.shape, 1)
    logits = jnp.where(lane < n_experts, _dot(h, wr_ref[...]), NEG_INF)

    def take_max(vals):
        top = jnp.max(vals, axis=-1, keepdims=True)
        idx = jnp.min(jnp.where(vals == top, lane, LANES), axis=-1, keepdims=True)
        return top, idx

    v1, i1 = take_max(logits)
    v2, i2 = take_max(jnp.where(lane == i1, NEG_INF, logits))
    e2 = jnp.exp(v2 - v1)
    w1 = 1.0 / (1.0 + e2)
    comb_ref[...] = jnp.where(lane == i1, w1, 0.0) + jnp.where(lane == i2, e2 * w1, 0.0)


def _router(x, g, w_router, tm):
    n, d = x.shape
    n_experts = w_router.shape[1]
    wr = jnp.pad(w_router, ((0, 0), (0, LANES - n_experts))).astype(BF16)
    return pl.pallas_call(
        functools.partial(_router_kernel, n_experts=n_experts),
        grid=(n // tm,),
        in_specs=[pl.BlockSpec((tm, d), lambda i: (i, 0)), pl.BlockSpec((1, d), lambda i: (0, 0)),
                  pl.BlockSpec((d, LANES), lambda i: (0, 0))],
        out_specs=pl.BlockSpec((tm, LANES), lambda i: (i, 0)),
        out_shape=jax.ShapeDtypeStruct((n, LANES), F32),
        compiler_params=_params("parallel"),
        name="router",
    )(x, g, wr)


def _ffn_kernel(x_ref, g_ref, comb_ref, wg_ref, wu_ref, wd_ref, out_ref, h_ref, acc_ref):
    e = pl.program_id(1)
    j = pl.program_id(2)

    @pl.when((e == 0) & (j == 0))
    def _():
        h_ref[...] = _rms(x_ref[...], g_ref[...]).astype(BF16)
        acc_ref[...] = jnp.zeros_like(acc_ref)

    h = h_ref[...]
    hg = _dot(h, wg_ref[0])
    act = (hg * _sigmoid(hg) * _dot(h, wu_ref[0])).astype(BF16)
    lane = lax.broadcasted_iota(jnp.int32, comb_ref.shape, 1)
    weight = jnp.sum(jnp.where(lane == e, comb_ref[...], 0.0), axis=-1, keepdims=True)
    acc_ref[...] += weight * _dot(act, wd_ref[0])

    @pl.when((e == pl.num_programs(1) - 1) & (j == pl.num_programs(2) - 1))
    def _():
        out_ref[...] = x_ref[...] + acc_ref[...]


def _ffn(x, g, comb, w_gu, w_down, tm, tf):
    n, d = x.shape
    n_e, d_ff = w_down.shape[0], w_down.shape[1]
    n_f = d_ff // tf
    return pl.pallas_call(
        _ffn_kernel,
        grid=(n // tm, n_e, n_f),
        in_specs=[pl.BlockSpec((tm, d), lambda i, e, j: (i, 0)),
                  pl.BlockSpec((1, d), lambda i, e, j: (0, 0)),
                  pl.BlockSpec((tm, LANES), lambda i, e, j: (i, 0)),
                  pl.BlockSpec((1, d, tf), lambda i, e, j: (e, 0, j)),
                  pl.BlockSpec((1, d, tf), lambda i, e, j: (e, 0, j + n_f)),
                  pl.BlockSpec((1, tf, d), lambda i, e, j: (e, j, 0))],
        out_specs=pl.BlockSpec((tm, d), lambda i, e, j: (i, 0)),
        out_shape=jax.ShapeDtypeStruct((n, d), F32),
        scratch_shapes=[pltpu.VMEM((tm, d), BF16), pltpu.VMEM((tm, d), F32)],
        compiler_params=_params("parallel", "arbitrary", "arbitrary"),
        name="ffn",
    )(x, g, comb, w_gu, w_gu, w_down)


def _ple_kernel(x_ref, p_ref, g_ref, wg_ref, wp_ref, gf_ref, out_ref, *, final):
    x = x_ref[...]
    h = _rms(x, g_ref[...]).astype(BF16)
    y = x + _sigmoid(_dot(h, wg_ref[...])) * _dot(p_ref[...].astype(BF16), wp_ref[...])
    out_ref[...] = _rms(y, gf_ref[...]) if final else y


def _ple(x, p, g, w_gate, w_proj, g_final, final, tm):
    n, d = x.shape
    dp = p.shape[1]
    row = pl.BlockSpec((1, d), lambda i: (0, 0))
    return pl.pallas_call(
        functools.partial(_ple_kernel, final=final),
        grid=(n // tm,),
        in_specs=[pl.BlockSpec((tm, d), lambda i: (i, 0)), pl.BlockSpec((tm, dp), lambda i: (i, 0)), row,
                  pl.BlockSpec((d, d), lambda i: (0, 0)), pl.BlockSpec((dp, d), lambda i: (0, 0)), row],
        out_specs=pl.BlockSpec((tm, d), lambda i: (i, 0)),
        out_shape=jax.ShapeDtypeStruct((n, d), F32),
        compiler_params=_params("parallel"),
        name="ple",
    )(x, p, g, w_gate, w_proj, g_final)


def _tiles(n, seq):
    return min(512, n), min(1024, n), min(512, seq)


def _ff_tile(d_ff):
    for tf in (512, 896, 256, 128):
        if d_ff % tf == 0:
            return tf
    raise ValueError(f"unsupported hidden width {d_ff}")


def kernel(x, p, positions, norm_mix_g, w_in, conv_w, pool_w, pool_scale, attn_sinks, w_branch, w_out, norm_ffn_g, ffn_w_gu, ffn_w_down, moe_router, moe_w_gu, moe_w_down, norm_ple_g, ple_w_proj, ple_w_gate, final_norm_g):
    batch, seq, d = x.shape
    depth = w_in.shape[0]
    n = batch * seq
    tm_in, tm, tile = _tiles(n, seq)
    assert seq % tile == 0 and tile % WINDOW == 0 and n % tm == 0 and n % tm_in == 0
    row = lambda a: a.reshape(1, -1)

    cos, sin = _rope_tables(positions, tm)
    ones = jnp.ones((n, LANES), F32)
    xs = x.reshape(n, d)
    for i in range(depth):
        conv, pq, kv, gates = _in_proj(xs, row(norm_mix_g[i]), cos, sin, w_in[i].astype(BF16), tm_in)
        xs = _mixer(xs, conv, pq, kv, gates, attn_sinks[i], conv_w[i], pool_w[i].astype(BF16),
                    row(pool_scale[i]), w_branch[i].astype(BF16), w_out[i].astype(BF16), batch, tile)
        g_ffn = row(norm_ffn_g[i])
        if i % 2 == 0:
            w_gu, w_down = ffn_w_gu[i // 2][None].astype(BF16), ffn_w_down[i // 2][None].astype(BF16)
            xs = _ffn(xs, g_ffn, ones, w_gu, w_down, tm, _ff_tile(w_down.shape[1]))
        else:
            comb = _router(xs, g_ffn, moe_router[i // 2], tm)
            w_gu, w_down = moe_w_gu[i // 2].astype(BF16), moe_w_down[i // 2].astype(BF16)
            xs = _ffn(xs, g_ffn, comb, w_gu, w_down, tm, _ff_tile(w_down.shape[1]))
        xs = _ple(xs, p[i].reshape(n, -1), row(norm_ple_g[i]), ple_w_gate[i].astype(BF16),
                  ple_w_proj[i].astype(BF16), row(final_norm_g), i == depth - 1, tm)
    return xs.reshape(batch, seq, d)
```

```python
import functools

import jax
import jax.numpy as jnp
from jax import lax
from jax.experimental import pallas as pl
from jax.experimental.pallas import tpu as pltpu

F32 = jnp.float32
BF16 = jnp.bfloat16

BRANCH_W = 512
N_BRANCH = 3
POOL_WINDOWS = (2, 4, 8, 16)
POOL_GROUP_W = 128
N_HEADS = 8
N_KV_HEADS = 2
HEADS_PER_KV = N_HEADS // N_KV_HEADS
HEAD_DIM = 64
HALF = HEAD_DIM // 2
KV_W = N_KV_HEADS * HEAD_DIM
WINDOW = 128
ROPE_THETA = 10000.0
NEG_INF = -1e30
EPS = 1e-6
OFF_POOL = 3 * BRANCH_W
OFF_Q = OFF_POOL + BRANCH_W
OFF_K = OFF_Q + N_HEADS * HEAD_DIM
OFF_GATE = OFF_K + 2 * KV_W

LANES = 128
V7X_VMEM_LIMIT_BYTES = 56 * 1024 * 1024
HALO = 16


def _params(*sem):
    return pltpu.CompilerParams(dimension_semantics=sem, vmem_limit_bytes=V7X_VMEM_LIMIT_BYTES)


def _rms(x, g):
    ms = jnp.mean(x * x, axis=-1, keepdims=True)
    return x * lax.rsqrt(ms + EPS) * g


def _sigmoid(x):
    return 1.0 / (1.0 + jnp.exp(-x))


def _dot(a, b):
    return jnp.dot(a, b, preferred_element_type=F32)


def _rope_kernel(pos_ref, freq_ref, sign_ref, cos_ref, sin_ref):
    ang = pos_ref[...].astype(F32) * freq_ref[...]
    cos_ref[...] = jnp.cos(ang)
    sin_ref[...] = jnp.sin(ang) * sign_ref[...]


def _rope_tables(positions, tm):
    n = positions.size
    inv_freq = ROPE_THETA ** (-2.0 * jnp.arange(HALF, dtype=F32) / HEAD_DIM)
    freq = jnp.tile(inv_freq, LANES // HALF).reshape(1, LANES)
    sign = jnp.tile(jnp.concatenate([-jnp.ones(HALF, F32), jnp.ones(HALF, F32)]), LANES // HEAD_DIM).reshape(1, LANES)
    row = pl.BlockSpec((1, LANES), lambda i: (0, 0))
    tab = pl.BlockSpec((tm, LANES), lambda i: (i, 0))
    return pl.pallas_call(
        _rope_kernel,
        grid=(n // tm,),
        in_specs=[pl.BlockSpec((tm, 1), lambda i: (i, 0)), row, row],
        out_specs=[tab, tab],
        out_shape=[jax.ShapeDtypeStruct((n, LANES), F32)] * 2,
        compiler_params=_params("parallel"),
        name="rope_tables",
    )(positions.reshape(n, 1), freq, sign)


def _rotate_half(x, cos, sin_signed):
    width = x.shape[-1]
    lane = lax.broadcasted_iota(jnp.int32, x.shape, 1)
    first = (lane % HEAD_DIM) < HALF
    partner = jnp.where(first, pltpu.roll(x, width - HALF, 1), pltpu.roll(x, HALF, 1))
    return x * cos + partner * sin_signed


def _in_proj_kernel(x_ref, g_ref, cos_ref, sin_ref, w_ref, conv_ref, pq_ref, kv_ref, gate_ref):
    h = _rms(x_ref[...], g_ref[...]).astype(BF16)

    def proj(c0, width):
        return _dot(h, w_ref[:, c0:c0 + width])

    for c in range(0, OFF_POOL, BRANCH_W):
        conv_ref[:, c:c + BRANCH_W] = proj(c, BRANCH_W).astype(BF16)
    pq_ref[:, :BRANCH_W] = proj(OFF_POOL, BRANCH_W).astype(BF16)
    cos = cos_ref[...]
    sin = sin_ref[...]
    rep = N_HEADS * HEAD_DIM // LANES
    q = _rotate_half(proj(OFF_Q, N_HEADS * HEAD_DIM), jnp.concatenate([cos] * rep, axis=1),
                     jnp.concatenate([sin] * rep, axis=1))
    pq_ref[:, BRANCH_W:] = (q * HEAD_DIM ** -0.5).astype(BF16)
    kv = proj(OFF_K, 2 * KV_W)
    kv_ref[:, :KV_W] = _rotate_half(kv[:, :KV_W], cos, sin).astype(BF16)
    kv_ref[:, KV_W:] = kv[:, KV_W:].astype(BF16)
    d = gate_ref.shape[1] // N_BRANCH
    for r in range(N_BRANCH):
        gate_ref[:, r * d:(r + 1) * d] = proj(OFF_GATE + r * d, d).astype(BF16)


def _in_proj(x, g, cos, sin, w_in, tm):
    n, d = x.shape
    d_in = w_in.shape[1]
    tok = lambda width: pl.BlockSpec((tm, width), lambda i: (i, 0))
    widths = (OFF_POOL, OFF_Q + N_HEADS * HEAD_DIM - OFF_POOL, 2 * KV_W, N_BRANCH * d)
    return pl.pallas_call(
        _in_proj_kernel,
        grid=(n // tm,),
        in_specs=[tok(d), pl.BlockSpec((1, d), lambda i: (0, 0)), tok(LANES), tok(LANES),
                  pl.BlockSpec((d, d_in), lambda i: (0, 0), pipeline_mode=pl.Buffered(1))],
        out_specs=[tok(w) for w in widths],
        out_shape=[jax.ShapeDtypeStruct((n, w), BF16) for w in widths],
        compiler_params=_params("parallel"),
        name="in_proj",
    )(x, g, cos, sin, w_in)


def _block_diag_heads(t, lo, use_rolled):
    rolled = pltpu.roll(t, HEAD_DIM, 1)
    low = jnp.where(lo, rolled if use_rolled else t, 0.0).astype(BF16)
    high = jnp.where(lo, 0.0, t if use_rolled else rolled).astype(BF16)
    z = jnp.zeros_like(low)
    return jnp.concatenate([
        jnp.concatenate([low, z], axis=1), jnp.concatenate([high, z], axis=1),
        jnp.concatenate([z, low], axis=1), jnp.concatenate([z, high], axis=1)], axis=0)


def _mixer_kernel(sinks_ref, conv_ref, convh_ref, pq_ref, poolh_ref, kv_ref, kvh_ref, gate_ref, x_ref,
                  convw_ref, poolw_ref, pscale_ref, wbr_ref, wout_ref, out_ref, att_ref, *, tile):
    seq_tile = pl.program_id(1)
    not_first = seq_tile > 0
    w = BRANCH_W

    v = conv_ref[:, w:2 * w].astype(F32) * conv_ref[:, 2 * w:3 * w].astype(F32)
    vh = convh_ref[:, w:2 * w].astype(F32) * convh_ref[:, 2 * w:3 * w].astype(F32)
    ext = jnp.concatenate([jnp.where(not_first, vh, 0.0), v], axis=0)
    cw = convw_ref[...]
    y = cw[2:3] * v + cw[1:2] * pltpu.roll(ext, 1, 0)[HALO:] + cw[0:1] * pltpu.roll(ext, 2, 0)[HALO:]
    y_conv = (conv_ref[:, :w].astype(F32) * y).astype(BF16)

    pu = pq_ref[:, :w].astype(F32)
    s = jnp.concatenate([jnp.where(not_first, poolh_ref[...].astype(F32), 0.0), pu], axis=0)
    pos1 = seq_tile * tile + lax.broadcasted_iota(jnp.int32, (tile, 1), 0) + 1
    mixed = []
    for gi, win in enumerate(POOL_WINDOWS):
        s = s[:, (POOL_GROUP_W if gi else 0):]
        s = s + pltpu.roll(s, win // 2, 0)
        count = jnp.minimum(pos1, win).astype(F32)
        lo_c = gi * POOL_GROUP_W
        pooled = s[HALO:, :POOL_GROUP_W] / count - pu[:, lo_c:lo_c + POOL_GROUP_W]
        mixed.append(_dot(pooled.astype(BF16), poolw_ref[gi]))
    y_pool = (jnp.concatenate(mixed, axis=1) * pscale_ref[...]).astype(BF16)

    kvh = jnp.where(not_first, kvh_ref[...], jnp.zeros_like(kvh_ref[...]))
    kext = jnp.concatenate([kvh, kv_ref[...]], axis=0)
    lo = lax.broadcasted_iota(jnp.int32, (2 * WINDOW, KV_W), 1) < HEAD_DIM
    row = lax.broadcasted_iota(jnp.int32, (WINDOW, 2 * WINDOW), 0)
    key = lax.broadcasted_iota(jnp.int32, (WINDOW, 2 * WINDOW), 1)
    dist = row + WINDOW - key
    in_band = (dist >= 0) & (dist < WINDOW)
    for j in range(tile // WINDOW):
        band = kext[j * WINDOW:(j + 2) * WINDOW].astype(F32)
        valid = in_band if j else in_band & ((key >= WINDOW) | not_first)
        for g in range(N_KV_HEADS):
            k_bd = _block_diag_heads(band[:, :KV_W], lo, g == 1)
            v_bd = _block_diag_heads(band[:, KV_W:], lo, g == 1)
            qw = HEADS_PER_KV * HEAD_DIM
            qg = pq_ref[j * WINDOW:(j + 1) * WINDOW, w + g * qw:w + (g + 1) * qw]
            scores = lax.dot_general(qg, k_bd, (((1,), (1,)), ((), ())), preferred_element_type=F32)
            probs = []
            for hh in range(HEADS_PER_KV):
                sh = jnp.where(valid, scores[:, hh * 2 * WINDOW:(hh + 1) * 2 * WINDOW], NEG_INF)
                sink = sinks_ref[g * HEADS_PER_KV + hh]
                m = jnp.maximum(jnp.max(sh, axis=-1, keepdims=True), sink)
                pexp = jnp.exp(sh - m)
                denom = jnp.sum(pexp, axis=-1, keepdims=True) + jnp.exp(sink - m)
                probs.append((pexp * (1.0 / denom)).astype(BF16))
            out = _dot(jnp.concatenate(probs, axis=1), v_bd)
            att_ref[j * WINDOW:(j + 1) * WINDOW, g * qw:(g + 1) * qw] = out.astype(BF16)

    d = x_ref.shape[1]
    merged = None
    for r, br in enumerate((y_conv, y_pool, att_ref[...])):
        term = _sigmoid(gate_ref[:, r * d:(r + 1) * d].astype(F32)) * _dot(br, wbr_ref[r])
        merged = term if merged is None else merged + term
    out_ref[...] = x_ref[...] + _dot(merged.astype(BF16), wout_ref[...])


def _mixer(x, conv, pq, kv, gates, sinks, conv_w, pool_w, pool_scale, w_branch, w_out, batch, tile):
    n, d = x.shape
    n_t = n // batch // tile
    t_idx = lambda b, i: b * n_t + i
    tok = lambda width: pl.BlockSpec((tile, width), lambda b, i: (t_idx(b, i), 0))
    prev = lambda rows, width: pl.BlockSpec(
        (rows, width), lambda b, i: (jnp.maximum(t_idx(b, i) * (tile // rows) - 1, 0), 0))
    full = lambda shape: pl.BlockSpec(shape, lambda b, i: (0,) * len(shape))
    return pl.pallas_call(
        functools.partial(_mixer_kernel, tile=tile),
        grid=(batch, n_t),
        in_specs=[pl.BlockSpec(memory_space=pltpu.SMEM),
                  tok(conv.shape[1]), prev(HALO, conv.shape[1]),
                  tok(pq.shape[1]), prev(HALO, BRANCH_W),
                  tok(kv.shape[1]), prev(WINDOW, kv.shape[1]),
                  tok(gates.shape[1]), tok(d),
                  full(conv_w.shape), full(pool_w.shape), full(pool_scale.shape),
                  full(w_branch.shape), full(w_out.shape)],
        out_specs=tok(d),
        out_shape=jax.ShapeDtypeStruct((n, d), F32),
        scratch_shapes=[pltpu.VMEM((tile, N_HEADS * HEAD_DIM), BF16)],
        compiler_params=_params("parallel", "parallel"),
        name="mixer",
    )(sinks, conv, conv, pq, pq, kv, kv, gates, x, conv_w, pool_w, pool_scale, w_branch, w_out)


def _ffn_kernel(x_ref, g_ref, wg_ref, wu_ref, wd_ref, out_ref, h_ref, acc_ref):
    j = pl.program_id(1)

    @pl.when(j == 0)
    def _():
        h_ref[...] = _rms(x_ref[...], g_ref[...]).astype(BF16)
        acc_ref[...] = jnp.zeros_like(acc_ref)

    h = h_ref[...]
    hg = _dot(h, wg_ref[...])
    act = (hg * _sigmoid(hg) * _dot(h, wu_ref[...])).astype(BF16)
    acc_ref[...] += _dot(act, wd_ref[...])

    @pl.when(j == pl.num_programs(1) - 1)
    def _():
        out_ref[...] = x_ref[...] + acc_ref[...]


def _ffn(x, g, w_gu, w_down, tm, tf):
    n, d = x.shape
    n_f = w_down.shape[0] // tf
    return pl.pallas_call(
        _ffn_kernel,
        grid=(n // tm, n_f),
        in_specs=[pl.BlockSpec((tm, d), lambda i, j: (i, 0)),
                  pl.BlockSpec((1, d), lambda i, j: (0, 0)),
                  pl.BlockSpec((d, tf), lambda i, j: (0, j)),
                  pl.BlockSpec((d, tf), lambda i, j: (0, j + n_f)),
                  pl.BlockSpec((tf, d), lambda i, j: (j, 0))],
        out_specs=pl.BlockSpec((tm, d), lambda i, j: (i, 0)),
        out_shape=jax.ShapeDtypeStruct((n, d), F32),
        scratch_shapes=[pltpu.VMEM((tm, d), BF16), pltpu.VMEM((tm, d), F32)],
        compiler_params=_params("parallel", "arbitrary"),
        name="ffn",
    )(x, g, w_gu, w_gu, w_down)


R_EXPERT, R_RANK, R_WEIGHT = 0, 2, 4


def _router_kernel(x_ref, g_ref, wr_ref, route_ref, count_ref, base_ref, *, n_experts):
    @pl.when(pl.program_id(0) == 0)
    def _():
        base_ref[...] = jnp.zeros_like(base_ref)

    tm = x_ref.shape[0]
    h = _rms(x_ref[...], g_ref[...]).astype(BF16)
    lane_i = lax.broadcasted_iota(jnp.int32, (tm, LANES), 1)
    lane = lane_i.astype(F32)
    logits = jnp.where(lane_i < n_experts, _dot(h, wr_ref[...]), NEG_INF)

    def take_max(vals):
        top = jnp.max(vals, axis=-1, keepdims=True)
        idx = jnp.min(jnp.where(vals == top, lane, float(LANES)), axis=-1, keepdims=True)
        return top, idx

    v1, i1 = take_max(logits)
    v2, i2 = take_max(jnp.where(lane == i1, NEG_INF, logits))
    e2 = jnp.exp(v2 - v1)
    w1 = 1.0 / (1.0 + e2)
    hit1, hit2 = lane == i1, lane == i2
    onehot = jnp.where(hit1 | hit2, 1.0, 0.0)
    earlier = (lax.broadcasted_iota(jnp.int32, (tm, tm), 0) > lax.broadcasted_iota(jnp.int32, (tm, tm), 1))
    before = _dot(earlier.astype(BF16), onehot.astype(BF16)) + base_ref[...]
    r1 = jnp.sum(jnp.where(hit1, before, 0.0), axis=-1, keepdims=True)
    r2 = jnp.sum(jnp.where(hit2, before, 0.0), axis=-1, keepdims=True)
    base_ref[...] += jnp.sum(onehot, axis=0, keepdims=True)
    count_ref[...] = base_ref[...]
    fields = (i1, i2, r1, r2, w1, e2 * w1)
    route = jnp.zeros((tm, LANES), F32)
    for k, val in enumerate(fields):
        route = jnp.where(lane_i == k, val, route)
    route_ref[...] = route


def _router(x, g, w_router, tm):
    n, d = x.shape
    n_experts = w_router.shape[1]
    wr = jnp.pad(w_router, ((0, 0), (0, LANES - n_experts))).astype(BF16)
    return pl.pallas_call(
        functools.partial(_router_kernel, n_experts=n_experts),
        grid=(n // tm,),
        in_specs=[pl.BlockSpec((tm, d), lambda i: (i, 0)), pl.BlockSpec((1, d), lambda i: (0, 0)),
                  pl.BlockSpec((d, LANES), lambda i: (0, 0))],
        out_specs=[pl.BlockSpec((tm, LANES), lambda i: (i, 0)), pl.BlockSpec((1, LANES), lambda i: (0, 0))],
        out_shape=[jax.ShapeDtypeStruct((n, LANES), F32), jax.ShapeDtypeStruct((1, LANES), F32)],
        scratch_shapes=[pltpu.VMEM((1, LANES), F32)],
        compiler_params=_params("arbitrary"),
        name="router",
    )(x, g, wr)


def _row_copy(src, src_row, dst, dst_row, sem):
    return pltpu.make_async_copy(src.at[pl.ds(src_row, 1)], dst.at[pl.ds(dst_row, 1)], sem)


def _dispatch_kernel(pad_ref, dest_ref, x_ref, g_ref, xs_ref, h_ref, zero_ref, sem, *, n_experts, unroll):
    tm = x_ref.shape[0]
    fill = lambda e: pltpu.make_async_copy(
        zero_ref, xs_ref.at[pl.ds(pl.multiple_of(pad_ref[e], 8), zero_ref.shape[0])], sem)

    @pl.when(pl.program_id(0) == 0)
    def _():
        zero_ref[...] = jnp.zeros_like(zero_ref)
        for e in range(n_experts):
            fill(e).start()
        for e in range(n_experts):
            fill(e).wait()

    h_ref[...] = _rms(x_ref[...], g_ref[...])

    def issue(r, carry):
        for k in range(2):
            _row_copy(h_ref, r, xs_ref, dest_ref[0, 0, 2 * r + k], sem).start()
        return carry

    def drain(r, carry):
        for k in range(2):
            _row_copy(h_ref, r, xs_ref, dest_ref[0, 0, 2 * r + k], sem).wait()
        return carry

    lax.fori_loop(0, tm, issue, 0, unroll=unroll)
    lax.fori_loop(0, tm, drain, 0, unroll=unroll)


def _dispatch(x, g, dest, pad_start, rows, tm, tmg):
    n, d = x.shape
    n_experts = pad_start.shape[0]
    return pl.pallas_call(
        functools.partial(_dispatch_kernel, n_experts=n_experts, unroll=8),
        grid_spec=pltpu.PrefetchScalarGridSpec(
            num_scalar_prefetch=1,
            grid=(n // tm,),
            in_specs=[pl.BlockSpec((1, 1, 2 * tm), lambda i, pad: (i, 0, 0), memory_space=pltpu.SMEM),
                      pl.BlockSpec((tm, d), lambda i, pad: (i, 0)),
                      pl.BlockSpec((1, d), lambda i, pad: (0, 0))],
            out_specs=pl.BlockSpec(memory_space=pl.ANY),
            scratch_shapes=[pltpu.VMEM((tm, d), F32), pltpu.VMEM((tmg, d), F32), pltpu.SemaphoreType.DMA(())],
        ),
        out_shape=jax.ShapeDtypeStruct((rows + tmg, d), F32),
        compiler_params=_params("arbitrary"),
        name="dispatch",
    )(pad_start, dest.reshape(n // tm, 1, 2 * tm), x, g)


def _expert_kernel(te_ref, used_ref, xs_ref, wg_ref, wu_ref, wd_ref, y_ref):
    @pl.when(pl.program_id(0) < used_ref[0])
    def _():
        h = xs_ref[...].astype(BF16)
        hg = _dot(h, wg_ref[0])
        act = (hg * _sigmoid(hg) * _dot(h, wu_ref[0])).astype(BF16)
        y_ref[...] = _dot(act, wd_ref[0])


def _experts(xs, tile_expert, n_used, w_gu, w_down, rows, tmg):
    d = xs.shape[1]
    d_ff = w_down.shape[1]
    live = lambda t, te, used: jnp.minimum(t, used[0] - 1)
    return pl.pallas_call(
        _expert_kernel,
        grid_spec=pltpu.PrefetchScalarGridSpec(
            num_scalar_prefetch=2,
            grid=(rows // tmg,),
            in_specs=[pl.BlockSpec((tmg, d), lambda t, te, used: (live(t, te, used), 0)),
                      pl.BlockSpec((1, d, d_ff), lambda t, te, used: (te[live(t, te, used)], 0, 0)),
                      pl.BlockSpec((1, d, d_ff), lambda t, te, used: (te[live(t, te, used)], 0, 1)),
                      pl.BlockSpec((1, d_ff, d), lambda t, te, used: (te[live(t, te, used)], 0, 0))],
            out_specs=pl.BlockSpec((tmg, d), lambda t, te, used: (live(t, te, used), 0)),
        ),
        out_shape=jax.ShapeDtypeStruct((rows, d), F32),
        compiler_params=_params("arbitrary"),
        name="experts",
    )(tile_expert, n_used, xs, w_gu, w_gu, w_down)


def _combine_kernel(dest_ref, x_ref, route_ref, y_ref, out_ref, buf_ref, sem, *, unroll):
    tm = x_ref.shape[0]

    def issue(r, carry):
        for k in range(2):
            _row_copy(y_ref, dest_ref[0, 0, 2 * r + k], buf_ref.at[k], r, sem).start()
        return carry

    def drain(r, carry):
        for k in range(2):
            _row_copy(y_ref, dest_ref[0, 0, 2 * r + k], buf_ref.at[k], r, sem).wait()
        return carry

    lax.fori_loop(0, tm, issue, 0, unroll=unroll)
    lax.fori_loop(0, tm, drain, 0, unroll=unroll)
    route = route_ref[...]
    out_ref[...] = (x_ref[...] + route[:, R_WEIGHT:R_WEIGHT + 1] * buf_ref[0]
                    + route[:, R_WEIGHT + 1:R_WEIGHT + 2] * buf_ref[1])


def _combine(x, route, dest, y, tm):
    n, d = x.shape
    return pl.pallas_call(
        functools.partial(_combine_kernel, unroll=8),
        grid=(n // tm,),
        in_specs=[pl.BlockSpec((1, 1, 2 * tm), lambda i: (i, 0, 0), memory_space=pltpu.SMEM),
                  pl.BlockSpec((tm, d), lambda i: (i, 0)),
                  pl.BlockSpec((tm, LANES), lambda i: (i, 0)),
                  pl.BlockSpec(memory_space=pl.ANY)],
        out_specs=pl.BlockSpec((tm, d), lambda i: (i, 0)),
        out_shape=jax.ShapeDtypeStruct((n, d), F32),
        scratch_shapes=[pltpu.VMEM((2, tm, d), F32), pltpu.SemaphoreType.DMA(())],
        compiler_params=_params("arbitrary"),
        name="combine",
    )(dest.reshape(n // tm, 1, 2 * tm), x, route, y)


def _moe(x, g, w_router, w_gu, w_down, tm, tmg):
    n = x.shape[0]
    n_experts = w_router.shape[1]
    route, count = _router(x, g, w_router, tm)
    counts = count[0, :n_experts].astype(jnp.int32)
    padded = (counts + tmg - 1) // tmg * tmg
    ends = jnp.cumsum(padded)
    starts = ends - padded
    expert = route[:, R_EXPERT:R_EXPERT + 2].astype(jnp.int32)
    rank = route[:, R_RANK:R_RANK + 2].astype(jnp.int32)
    ids = jnp.arange(n_experts, dtype=jnp.int32)
    dest = jnp.sum(jnp.where(expert[..., None] == ids, starts, 0), axis=-1) + rank
    rows = 2 * n + n_experts * tmg
    n_used = (ends[-1:] // tmg).astype(jnp.int32)
    tile_row = jnp.arange(rows // tmg, dtype=jnp.int32) * tmg
    tile_expert = jnp.minimum(jnp.sum(tile_row[:, None] >= ends[None, :], axis=-1), n_experts - 1).astype(jnp.int32)
    pad_start = ((starts + counts) // 8 * 8).astype(jnp.int32)

    xs = _dispatch(x, g, dest, pad_start, rows, tm, tmg)
    y = _experts(xs, tile_expert, n_used, w_gu, w_down, rows, tmg)
    return _combine(x, route, dest, y, tm)


def _ple_kernel(x_ref, p_ref, g_ref, wg_ref, wp_ref, gf_ref, out_ref, *, final):
    x = x_ref[...]
    h = _rms(x, g_ref[...]).astype(BF16)
    y = x + _sigmoid(_dot(h, wg_ref[...])) * _dot(p_ref[...].astype(BF16), wp_ref[...])
    out_ref[...] = _rms(y, gf_ref[...]) if final else y


def _ple(x, p, g, w_gate, w_proj, g_final, final, tm):
    n, d = x.shape
    dp = p.shape[1]
    row = pl.BlockSpec((1, d), lambda i: (0, 0))
    return pl.pallas_call(
        functools.partial(_ple_kernel, final=final),
        grid=(n // tm,),
        in_specs=[pl.BlockSpec((tm, d), lambda i: (i, 0)), pl.BlockSpec((tm, dp), lambda i: (i, 0)), row,
                  pl.BlockSpec((d, d), lambda i: (0, 0)), pl.BlockSpec((dp, d), lambda i: (0, 0)), row],
        out_specs=pl.BlockSpec((tm, d), lambda i: (i, 0)),
        out_shape=jax.ShapeDtypeStruct((n, d), F32),
        compiler_params=_params("parallel"),
        name="ple",
    )(x, p, g, w_gate, w_proj, g_final)


def _tiles(n, seq):
    return min(512, n), min(1024, n), min(512, seq), min(512, n), 512


def _ff_tile(d_ff):
    for tf in (512, 896, 256, 128):
        if d_ff % tf == 0:
            return tf
    raise ValueError(f"unsupported hidden width {d_ff}")


def kernel(x, p, positions, norm_mix_g, w_in, conv_w, pool_w, pool_scale, attn_sinks, w_branch, w_out, norm_ffn_g, ffn_w_gu, ffn_w_down, moe_router, moe_w_gu, moe_w_down, norm_ple_g, ple_w_proj, ple_w_gate, final_norm_g):
    batch, seq, d = x.shape
    depth = w_in.shape[0]
    n = batch * seq
    tm_in, tm, tile, tm_row, tmg = _tiles(n, seq)
    assert seq % tile == 0 and tile % WINDOW == 0 and n % tm == 0 and n % tm_in == 0 and n % tm_row == 0
    row = lambda a: a.reshape(1, -1)

    cos, sin = _rope_tables(positions, tm)
    xs = x.reshape(n, d)
    for i in range(depth):
        conv, pq, kv, gates = _in_proj(xs, row(norm_mix_g[i]), cos, sin, w_in[i].astype(BF16), tm_in)
        xs = _mixer(xs, conv, pq, kv, gates, attn_sinks[i], conv_w[i], pool_w[i].astype(BF16),
                    row(pool_scale[i]), w_branch[i].astype(BF16), w_out[i].astype(BF16), batch, tile)
        g_ffn = row(norm_ffn_g[i])
        if i % 2 == 0:
            w_down = ffn_w_down[i // 2].astype(BF16)
            xs = _ffn(xs, g_ffn, ffn_w_gu[i // 2].astype(BF16), w_down, tm, _ff_tile(w_down.shape[0]))
        else:
            xs = _moe(xs, g_ffn, moe_router[i // 2], moe_w_gu[i // 2].astype(BF16),
                      moe_w_down[i // 2].astype(BF16), tm_row, tmg)
        xs = _ple(xs, p[i].reshape(n, -1), row(norm_ple_g[i]), ple_w_gate[i].astype(BF16),
                  ple_w_proj[i].astype(BF16), row(final_norm_g), i == depth - 1, tm)
    return xs.reshape(batch, seq, d)
```

```python
import functools

import jax
import jax.numpy as jnp
from jax import lax
from jax.experimental import pallas as pl
from jax.experimental.pallas import tpu as pltpu

F32 = jnp.float32
BF16 = jnp.bfloat16

BRANCH_W = 512
N_BRANCH = 3
POOL_WINDOWS = (2, 4, 8, 16)
POOL_GROUP_W = 128
N_HEADS = 8
N_KV_HEADS = 2
HEADS_PER_KV = N_HEADS // N_KV_HEADS
HEAD_DIM = 64
HALF = HEAD_DIM // 2
KV_W = N_KV_HEADS * HEAD_DIM
WINDOW = 128
ROPE_THETA = 10000.0
NEG_INF = -1e30
EPS = 1e-6
OFF_POOL = 3 * BRANCH_W
OFF_Q = OFF_POOL + BRANCH_W
OFF_K = OFF_Q + N_HEADS * HEAD_DIM
OFF_GATE = OFF_K + 2 * KV_W

LANES = 128
V7X_VMEM_LIMIT_BYTES = 56 * 1024 * 1024
HALO = 16


def _params(*sem):
    return pltpu.CompilerParams(dimension_semantics=sem, vmem_limit_bytes=V7X_VMEM_LIMIT_BYTES)


def _rms(x, g):
    ms = jnp.mean(x * x, axis=-1, keepdims=True)
    return x * lax.rsqrt(ms + EPS) * g


def _sigmoid(x):
    return 0.5 * jnp.tanh(0.5 * x) + 0.5


def _dot(a, b):
    return jnp.dot(a, b, preferred_element_type=F32)


def _layer_row(stacked, layer):
    return pl.BlockSpec((None, 1, stacked.shape[2]), lambda *_: (layer, 0, 0))


def _layer_full(stacked, layer):
    shape = stacked.shape[1:]
    return pl.BlockSpec((None,) + shape, lambda *_: (layer,) + (0,) * len(shape))


def _rope_kernel(pos_ref, freq_ref, sign_ref, cos_ref, sin_ref):
    ang = pos_ref[...].astype(F32) * freq_ref[...]
    cos_ref[...] = jnp.cos(ang)
    sin_ref[...] = jnp.sin(ang) * sign_ref[...]


def _rope_tables(positions, tm):
    n = positions.size
    inv_freq = ROPE_THETA ** (-2.0 * jnp.arange(HALF, dtype=F32) / HEAD_DIM)
    freq = jnp.tile(inv_freq, LANES // HALF).reshape(1, LANES)
    sign = jnp.tile(jnp.concatenate([-jnp.ones(HALF, F32), jnp.ones(HALF, F32)]), LANES // HEAD_DIM).reshape(1, LANES)
    row = pl.BlockSpec((1, LANES), lambda i: (0, 0))
    tab = pl.BlockSpec((tm, LANES), lambda i: (i, 0))
    return pl.pallas_call(
        _rope_kernel,
        grid=(n // tm,),
        in_specs=[pl.BlockSpec((tm, 1), lambda i: (i, 0)), row, row],
        out_specs=[tab, tab],
        out_shape=[jax.ShapeDtypeStruct((n, LANES), F32)] * 2,
        compiler_params=_params("parallel"),
        name="rope_tables",
    )(positions.reshape(n, 1), freq, sign)


def _rotate_half(x, cos, sin_signed):
    width = x.shape[-1]
    lane = lax.broadcasted_iota(jnp.int32, x.shape, 1)
    first = (lane % HEAD_DIM) < HALF
    partner = jnp.where(first, pltpu.roll(x, width - HALF, 1), pltpu.roll(x, HALF, 1))
    return x * cos + partner * sin_signed


def _in_proj_kernel(x_ref, g_ref, cos_ref, sin_ref, w_ref, conv_ref, pq_ref, kv_ref, gate_ref):
    h = _rms(x_ref[...], g_ref[...]).astype(BF16)

    def proj(c0, width):
        return _dot(h, w_ref[:, c0:c0 + width])

    conv_ref[:, :BRANCH_W] = proj(0, BRANCH_W).astype(BF16)
    conv_ref[:, BRANCH_W:] = (proj(BRANCH_W, BRANCH_W) * proj(2 * BRANCH_W, BRANCH_W)).astype(BF16)
    pq_ref[:, :BRANCH_W] = proj(OFF_POOL, BRANCH_W).astype(BF16)
    cos = cos_ref[...]
    sin = sin_ref[...]
    rep = N_HEADS * HEAD_DIM // LANES
    q = _rotate_half(proj(OFF_Q, N_HEADS * HEAD_DIM), jnp.concatenate([cos] * rep, axis=1),
                     jnp.concatenate([sin] * rep, axis=1))
    pq_ref[:, BRANCH_W:] = (q * HEAD_DIM ** -0.5).astype(BF16)
    kv = proj(OFF_K, 2 * KV_W)
    kv_ref[:, :KV_W] = _rotate_half(kv[:, :KV_W], cos, sin).astype(BF16)
    kv_ref[:, KV_W:] = kv[:, KV_W:].astype(BF16)
    d = gate_ref.shape[1] // N_BRANCH
    for r in range(N_BRANCH):
        gate_ref[:, r * d:(r + 1) * d] = _sigmoid(proj(OFF_GATE + r * d, d)).astype(BF16)


def _in_proj(x, g, cos, sin, w_in, layer, tm):
    n, d = x.shape
    d_in = w_in.shape[2]
    tok = lambda width: pl.BlockSpec((tm, width), lambda i: (i, 0))
    widths = (2 * BRANCH_W, OFF_Q + N_HEADS * HEAD_DIM - OFF_POOL, 2 * KV_W, N_BRANCH * d)
    return pl.pallas_call(
        _in_proj_kernel,
        grid=(n // tm,),
        in_specs=[tok(d), _layer_row(g, layer), tok(LANES), tok(LANES),
                  pl.BlockSpec((None, d, d_in), lambda i: (layer, 0, 0), pipeline_mode=pl.Buffered(1))],
        out_specs=[tok(w) for w in widths],
        out_shape=[jax.ShapeDtypeStruct((n, w), BF16) for w in widths],
        compiler_params=_params("parallel"),
        name="in_proj",
    )(x, g, cos, sin, w_in)


def _block_diag_heads(t, lo, use_rolled):
    rolled = pltpu.roll(t, HEAD_DIM, 1)
    low = jnp.where(lo, rolled if use_rolled else t, 0.0).astype(BF16)
    high = jnp.where(lo, 0.0, t if use_rolled else rolled).astype(BF16)
    z = jnp.zeros_like(low)
    return jnp.concatenate([
        jnp.concatenate([low, z], axis=1), jnp.concatenate([high, z], axis=1),
        jnp.concatenate([z, low], axis=1), jnp.concatenate([z, high], axis=1)], axis=0)


def _mixer_kernel(sinks_ref, conv_ref, convh_ref, pq_ref, poolh_ref, kv_ref, kvh_ref, gate_ref, x_ref,
                  convw_ref, poolw_ref, pscale_ref, wbr_ref, wout_ref, out_ref, att_ref, *, tile, layer):
    seq_tile = pl.program_id(1)
    not_first = seq_tile > 0
    w = BRANCH_W

    v = conv_ref[:, w:].astype(F32)
    vh = convh_ref[:, w:].astype(F32)
    ext = jnp.concatenate([jnp.where(not_first, vh, 0.0), v], axis=0)
    cw = convw_ref[...]
    y = cw[2:3] * v + cw[1:2] * pltpu.roll(ext, 1, 0)[HALO:] + cw[0:1] * pltpu.roll(ext, 2, 0)[HALO:]
    y_conv = (conv_ref[:, :w].astype(F32) * y).astype(BF16)

    pu = pq_ref[:, :w].astype(F32)
    s = jnp.concatenate([jnp.where(not_first, poolh_ref[...].astype(F32), 0.0), pu], axis=0)
    pos1 = seq_tile * tile + lax.broadcasted_iota(jnp.int32, (tile, 1), 0) + 1
    mixed = []
    for gi, win in enumerate(POOL_WINDOWS):
        s = s[:, (POOL_GROUP_W if gi else 0):]
        s = s + pltpu.roll(s, win // 2, 0)
        count = jnp.minimum(pos1, win).astype(F32)
        lo_c = gi * POOL_GROUP_W
        pooled = s[HALO:, :POOL_GROUP_W] / count - pu[:, lo_c:lo_c + POOL_GROUP_W]
        mixed.append(_dot(pooled.astype(BF16), poolw_ref[gi]))
    y_pool = (jnp.concatenate(mixed, axis=1) * pscale_ref[...]).astype(BF16)

    kvh = jnp.where(not_first, kvh_ref[...], jnp.zeros_like(kvh_ref[...]))
    kext = jnp.concatenate([kvh, kv_ref[...]], axis=0)
    qw = HEADS_PER_KV * HEAD_DIM
    rows = HEADS_PER_KV * WINDOW
    lo = lax.broadcasted_iota(jnp.int32, (2 * WINDOW, KV_W), 1) < HEAD_DIM
    lane_head = lax.broadcasted_iota(jnp.int32, (WINDOW, qw), 1) // HEAD_DIM
    qi = lax.broadcasted_iota(jnp.int32, (rows, 2 * WINDOW), 0) % WINDOW
    key = lax.broadcasted_iota(jnp.int32, (rows, 2 * WINDOW), 1)
    dist = qi + WINDOW - key
    in_band = (dist >= 0) & (dist < WINDOW)
    row_head = lax.broadcasted_iota(jnp.int32, (rows, 1), 0) // WINDOW
    for j in range(tile // WINDOW):
        band = kext[j * WINDOW:(j + 2) * WINDOW].astype(F32)
        halves = [(t, pltpu.roll(t, HEAD_DIM, 1)) for t in (band[:, :KV_W], band[:, KV_W:])]
        valid = in_band if j else in_band & ((key >= WINDOW) | not_first)
        for g in range(N_KV_HEADS):
            k_rep, v_rep = (
                jnp.concatenate([jnp.where(lo, r, t) if g else jnp.where(lo, t, r)] * (qw // KV_W), axis=1).astype(BF16)
                for t, r in halves)
            qg = pq_ref[j * WINDOW:(j + 1) * WINDOW, w + g * qw:w + (g + 1) * qw]
            q_heads = jnp.concatenate(
                [jnp.where(lane_head == hh, qg, jnp.zeros_like(qg)) for hh in range(HEADS_PER_KV)], axis=0)
            scores = lax.dot_general(q_heads, k_rep, (((1,), (1,)), ((), ())), preferred_element_type=F32)
            sink = jnp.zeros((rows, 1), F32)
            for hh in range(HEADS_PER_KV):
                sink = jnp.where(row_head == hh, sinks_ref[layer, g * HEADS_PER_KV + hh], sink)
            sh = jnp.where(valid, scores, NEG_INF)
            m = jnp.maximum(jnp.max(sh, axis=-1, keepdims=True), sink)
            pexp = jnp.exp(sh - m)
            denom = jnp.sum(pexp, axis=-1, keepdims=True) + jnp.exp(sink - m)
            res = _dot((pexp * (1.0 / denom)).astype(BF16), v_rep)
            out = res[:WINDOW]
            for hh in range(1, HEADS_PER_KV):
                out = jnp.where(lane_head == hh, res[hh * WINDOW:(hh + 1) * WINDOW], out)
            att_ref[j * WINDOW:(j + 1) * WINDOW, g * qw:(g + 1) * qw] = out.astype(BF16)

    d = x_ref.shape[1]
    merged = None
    for r, br in enumerate((y_conv, y_pool, att_ref[...])):
        term = gate_ref[:, r * d:(r + 1) * d].astype(F32) * _dot(br, wbr_ref[r])
        merged = term if merged is None else merged + term
    out_ref[...] = x_ref[...] + _dot(merged.astype(BF16), wout_ref[...])


def _mixer(x, conv, pq, kv, gates, sinks, conv_w, pool_w, pool_scale, w_branch, w_out, layer, batch, tile):
    n, d = x.shape
    n_t = n // batch // tile
    t_idx = lambda b, i: b * n_t + i
    tok = lambda width: pl.BlockSpec((tile, width), lambda b, i: (t_idx(b, i), 0))
    prev = lambda rows, width: pl.BlockSpec(
        (rows, width), lambda b, i: (jnp.maximum(t_idx(b, i) * (tile // rows) - 1, 0), 0))
    return pl.pallas_call(
        functools.partial(_mixer_kernel, tile=tile, layer=layer),
        grid=(batch, n_t),
        in_specs=[pl.BlockSpec(memory_space=pltpu.SMEM),
                  tok(conv.shape[1]), prev(HALO, conv.shape[1]),
                  tok(pq.shape[1]), prev(HALO, BRANCH_W),
                  tok(kv.shape[1]), prev(WINDOW, kv.shape[1]),
                  tok(gates.shape[1]), tok(d),
                  _layer_full(conv_w, layer), _layer_full(pool_w, layer), _layer_row(pool_scale, layer),
                  _layer_full(w_branch, layer), _layer_full(w_out, layer)],
        out_specs=tok(d),
        out_shape=jax.ShapeDtypeStruct((n, d), F32),
        scratch_shapes=[pltpu.VMEM((tile, N_HEADS * HEAD_DIM), BF16)],
        compiler_params=_params("parallel", "parallel"),
        name="mixer",
    )(sinks, conv, conv, pq, pq, kv, kv, gates, x, conv_w, pool_w, pool_scale, w_branch, w_out)


def _ffn_kernel(x_ref, g_ref, wg_ref, wu_ref, wd_ref, out_ref, h_ref, acc_ref):
    j = pl.program_id(1)

    @pl.when(j == 0)
    def _():
        h_ref[...] = _rms(x_ref[...], g_ref[...]).astype(BF16)
        acc_ref[...] = jnp.zeros_like(acc_ref)

    h = h_ref[...]
    hg = _dot(h, wg_ref[...])
    act = (hg * _sigmoid(hg) * _dot(h, wu_ref[...])).astype(BF16)
    acc_ref[...] += _dot(act, wd_ref[...])

    @pl.when(j == pl.num_programs(1) - 1)
    def _():
        out_ref[...] = x_ref[...] + acc_ref[...]


def _ffn(x, g, w_gu, w_down, layer, idx, tm, tf):
    n, d = x.shape
    n_f = w_down.shape[1] // tf
    return pl.pallas_call(
        _ffn_kernel,
        grid=(n // tm, n_f),
        in_specs=[pl.BlockSpec((tm, d), lambda i, j: (i, 0)),
                  _layer_row(g, layer),
                  pl.BlockSpec((None, d, tf), lambda i, j: (idx, 0, j)),
                  pl.BlockSpec((None, d, tf), lambda i, j: (idx, 0, j + n_f)),
                  pl.BlockSpec((None, tf, d), lambda i, j: (idx, j, 0))],
        out_specs=pl.BlockSpec((tm, d), lambda i, j: (i, 0)),
        out_shape=jax.ShapeDtypeStruct((n, d), F32),
        scratch_shapes=[pltpu.VMEM((tm, d), BF16), pltpu.VMEM((tm, d), F32)],
        compiler_params=_params("parallel", "arbitrary"),
        name="ffn",
    )(x, g, w_gu, w_gu, w_down)


R_EXPERT, R_RANK, R_WEIGHT = 0, 2, 4


def _router_kernel(x_ref, g_ref, wr_ref, route_ref, count_ref, base_ref, *, n_experts):
    @pl.when(pl.program_id(0) == 0)
    def _():
        base_ref[...] = jnp.zeros_like(base_ref)

    tm = x_ref.shape[0]
    h = _rms(x_ref[...], g_ref[...]).astype(BF16)
    lane_i = lax.broadcasted_iota(jnp.int32, (tm, LANES), 1)
    lane = lane_i.astype(F32)
    logits = jnp.where(lane_i < n_experts, _dot(h, wr_ref[...]), NEG_INF)

    def take_max(vals):
        top = jnp.max(vals, axis=-1, keepdims=True)
        idx = jnp.min(jnp.where(vals == top, lane, float(LANES)), axis=-1, keepdims=True)
        return top, idx

    v1, i1 = take_max(logits)
    v2, i2 = take_max(jnp.where(lane == i1, NEG_INF, logits))
    e2 = jnp.exp(v2 - v1)
    w1 = 1.0 / (1.0 + e2)
    hit1, hit2 = lane == i1, lane == i2
    onehot = jnp.where(hit1 | hit2, 1.0, 0.0)
    earlier = (lax.broadcasted_iota(jnp.int32, (tm, tm), 0) > lax.broadcasted_iota(jnp.int32, (tm, tm), 1))
    before = _dot(earlier.astype(BF16), onehot.astype(BF16)) + base_ref[...]
    r1 = jnp.sum(jnp.where(hit1, before, 0.0), axis=-1, keepdims=True)
    r2 = jnp.sum(jnp.where(hit2, before, 0.0), axis=-1, keepdims=True)
    base_ref[...] += jnp.sum(onehot, axis=0, keepdims=True)
    count_ref[...] = base_ref[...]
    fields = (i1, i2, r1, r2, w1, e2 * w1)
    route = jnp.zeros((tm, LANES), F32)
    for k, val in enumerate(fields):
        route = jnp.where(lane_i == k, val, route)
    route_ref[...] = route


def _router(x, g, w_router, layer, tm):
    n, d = x.shape
    n_experts = w_router.shape[1]
    wr = jnp.pad(w_router, ((0, 0), (0, LANES - n_experts))).astype(BF16)
    return pl.pallas_call(
        functools.partial(_router_kernel, n_experts=n_experts),
        grid=(n // tm,),
        in_specs=[pl.BlockSpec((tm, d), lambda i: (i, 0)), _layer_row(g, layer),
                  pl.BlockSpec((d, LANES), lambda i: (0, 0))],
        out_specs=[pl.BlockSpec((tm, LANES), lambda i: (i, 0)), pl.BlockSpec((1, LANES), lambda i: (0, 0))],
        out_shape=[jax.ShapeDtypeStruct((n, LANES), F32), jax.ShapeDtypeStruct((1, LANES), F32)],
        scratch_shapes=[pltpu.VMEM((1, LANES), F32)],
        compiler_params=_params("arbitrary"),
        name="router",
    )(x, g, wr)


def _row_copy(src, src_row, dst, dst_row, sem):
    return pltpu.make_async_copy(src.at[pl.ds(src_row, 1)], dst.at[pl.ds(dst_row, 1)], sem)


def _dispatch_kernel(pad_ref, dest_ref, x_ref, g_ref, xs_ref, h_ref, zero_ref, sem, *, n_experts, unroll):
    tm = x_ref.shape[0]
    tmg = zero_ref.shape[0]
    fill = lambda row: pltpu.make_async_copy(zero_ref, xs_ref.at[pl.ds(pl.multiple_of(row, 8), tmg)], sem)

    @pl.when(pl.program_id(0) == 0)
    def _():
        zero_ref[...] = jnp.zeros_like(zero_ref)
        for e in range(n_experts):
            fill(pad_ref[e]).start()
        for e in range(n_experts):
            fill(pad_ref[e]).wait()
        for t in range(n_experts + 1):
            row = pad_ref[n_experts] + t * tmg

            @pl.when(row < xs_ref.shape[0])
            def _():
                fill(row).start()
                fill(row).wait()

    h_ref[...] = _rms(x_ref[...], g_ref[...])

    def issue(r, carry):
        for k in range(2):
            _row_copy(h_ref, r, xs_ref, dest_ref[0, 0, 2 * r + k], sem).start(priority=k)
        return carry

    def drain(r, carry):
        for k in range(2):
            _row_copy(h_ref, r, xs_ref, dest_ref[0, 0, 2 * r + k], sem).wait()
        return carry

    lax.fori_loop(0, tm, issue, 0, unroll=unroll)
    lax.fori_loop(0, tm, drain, 0, unroll=unroll)


def _dispatch(x, g, layer, dest, pad_start, rows, tm, tmg):
    n, d = x.shape
    n_experts = pad_start.shape[0] - 1
    return pl.pallas_call(
        functools.partial(_dispatch_kernel, n_experts=n_experts, unroll=8),
        grid_spec=pltpu.PrefetchScalarGridSpec(
            num_scalar_prefetch=1,
            grid=(n // tm,),
            in_specs=[pl.BlockSpec((1, 1, 2 * tm), lambda i, pad: (i, 0, 0), memory_space=pltpu.SMEM),
                      pl.BlockSpec((tm, d), lambda i, pad: (i, 0)),
                      _layer_row(g, layer)],
            out_specs=pl.BlockSpec(memory_space=pl.ANY),
            scratch_shapes=[pltpu.VMEM((tm, d), F32), pltpu.VMEM((tmg, d), F32), pltpu.SemaphoreType.DMA(())],
        ),
        out_shape=jax.ShapeDtypeStruct((rows + tmg, d), F32),
        compiler_params=_params("arbitrary"),
        name="dispatch",
    )(pad_start, dest.reshape(n // tm, 1, 2 * tm), x, g)


def _expert_kernel(te_ref, used_ref, xs_ref, wg_ref, wu_ref, wd_ref, y_ref):
    @pl.when(pl.program_id(0) < used_ref[0])
    def _():
        h = xs_ref[...].astype(BF16)
        hg = _dot(h, wg_ref[0])
        act = (hg * _sigmoid(hg) * _dot(h, wu_ref[0])).astype(BF16)
        y_ref[...] = _dot(act, wd_ref[0])

    @pl.when(pl.program_id(0) >= used_ref[0])
    def _():
        y_ref[...] = jnp.zeros_like(y_ref)


def _experts(xs, tile_expert, n_used, w_gu, w_down, idx, rows, tmg):
    d = xs.shape[1]
    d_ff = w_down.shape[2]
    live = lambda t, te, used: jnp.minimum(t, used[0] - 1)
    return pl.pallas_call(
        _expert_kernel,
        grid_spec=pltpu.PrefetchScalarGridSpec(
            num_scalar_prefetch=2,
            grid=(rows // tmg,),
            in_specs=[pl.BlockSpec((tmg, d), lambda t, te, used: (live(t, te, used), 0)),
                      pl.BlockSpec((None, 1, d, d_ff), lambda t, te, used: (idx, te[live(t, te, used)], 0, 0)),
                      pl.BlockSpec((None, 1, d, d_ff), lambda t, te, used: (idx, te[live(t, te, used)], 0, 1)),
                      pl.BlockSpec((None, 1, d_ff, d), lambda t, te, used: (idx, te[live(t, te, used)], 0, 0))],
            out_specs=pl.BlockSpec((tmg, d), lambda t, te, used: (t, 0)),
        ),
        out_shape=jax.ShapeDtypeStruct((rows, d), F32),
        compiler_params=_params("arbitrary"),
        name="experts",
    )(tile_expert, n_used, xs, w_gu, w_gu, w_down)


def _combine_kernel(dest_ref, x_ref, route_ref, y_ref, out_ref, buf_ref, sem, *, unroll):
    tm = x_ref.shape[0]

    def issue(r, carry):
        for k in range(2):
            _row_copy(y_ref, dest_ref[0, 0, 2 * r + k], buf_ref.at[k], r, sem).start(priority=k)
        return carry

    def drain(r, carry):
        for k in range(2):
            _row_copy(y_ref, dest_ref[0, 0, 2 * r + k], buf_ref.at[k], r, sem).wait()
        return carry

    lax.fori_loop(0, tm, issue, 0, unroll=unroll)
    lax.fori_loop(0, tm, drain, 0, unroll=unroll)
    route = route_ref[...]
    out_ref[...] = (x_ref[...] + route[:, R_WEIGHT:R_WEIGHT + 1] * buf_ref[0]
                    + route[:, R_WEIGHT + 1:R_WEIGHT + 2] * buf_ref[1])


def _combine(x, route, dest, y, tm):
    n, d = x.shape
    return pl.pallas_call(
        functools.partial(_combine_kernel, unroll=8),
        grid=(n // tm,),
        in_specs=[pl.BlockSpec((1, 1, 2 * tm), lambda i: (i, 0, 0), memory_space=pltpu.SMEM),
                  pl.BlockSpec((tm, d), lambda i: (i, 0)),
                  pl.BlockSpec((tm, LANES), lambda i: (i, 0)),
                  pl.BlockSpec(memory_space=pl.ANY)],
        out_specs=pl.BlockSpec((tm, d), lambda i: (i, 0)),
        out_shape=jax.ShapeDtypeStruct((n, d), F32),
        scratch_shapes=[pltpu.VMEM((2, tm, d), F32), pltpu.SemaphoreType.DMA(())],
        compiler_params=_params("arbitrary"),
        name="combine",
    )(dest.reshape(n // tm, 1, 2 * tm), x, route, y)


def _moe(x, g, layer, w_router, w_gu, w_down, idx, tm, tmg):
    n = x.shape[0]
    n_experts = w_router.shape[1]
    route, count = _router(x, g, w_router, layer, tm)
    counts = count[0, :n_experts].astype(jnp.int32)
    padded = (counts + tmg - 1) // tmg * tmg
    ends = jnp.cumsum(padded)
    starts = ends - padded
    expert = route[:, R_EXPERT:R_EXPERT + 2].astype(jnp.int32)
    rank = route[:, R_RANK:R_RANK + 2].astype(jnp.int32)
    ids = jnp.arange(n_experts, dtype=jnp.int32)
    dest = jnp.sum(jnp.where(expert[..., None] == ids, starts, 0), axis=-1) + rank
    rows = 2 * n + n_experts * tmg
    n_used = (ends[-1:] // tmg).astype(jnp.int32)
    tile_row = jnp.arange(rows // tmg, dtype=jnp.int32) * tmg
    tile_expert = jnp.minimum(jnp.sum(tile_row[:, None] >= ends[None, :], axis=-1), n_experts - 1).astype(jnp.int32)
    pad_start = jnp.concatenate([(starts + counts) // 8 * 8, ends[-1:]]).astype(jnp.int32)

    xs = _dispatch(x, g, layer, dest, pad_start, rows, tm, tmg)
    y = _experts(xs, tile_expert, n_used, w_gu, w_down, idx, rows, tmg)
    return _combine(x, route, dest, y, tm)


def _ple_kernel(x_ref, p_ref, g_ref, wg_ref, wp_ref, gf_ref, out_ref, *, final):
    x = x_ref[...]
    h = _rms(x, g_ref[...]).astype(BF16)
    y = x + _sigmoid(_dot(h, wg_ref[...])) * _dot(p_ref[...].astype(BF16), wp_ref[...])
    out_ref[...] = _rms(y, gf_ref[...]) if final else y


def _ple(x, p, g, w_gate, w_proj, g_final, layer, final, tm):
    n, d = x.shape
    dp = p.shape[2]
    return pl.pallas_call(
        functools.partial(_ple_kernel, final=final),
        grid=(n // tm,),
        in_specs=[pl.BlockSpec((tm, d), lambda i: (i, 0)), pl.BlockSpec((None, tm, dp), lambda i: (layer, i, 0)),
                  _layer_row(g, layer), _layer_full(w_gate, layer), _layer_full(w_proj, layer),
                  pl.BlockSpec((1, d), lambda i: (0, 0))],
        out_specs=pl.BlockSpec((tm, d), lambda i: (i, 0)),
        out_shape=jax.ShapeDtypeStruct((n, d), F32),
        compiler_params=_params("parallel"),
        name="ple",
    )(x, p, g, w_gate, w_proj, g_final)


def _tiles(n, seq):
    return min(512, n), min(1024, n), min(512, seq), min(512, n), 512


def _ff_tile(d_ff):
    for tf in (512, 896, 256, 128):
        if d_ff % tf == 0:
            return tf
    raise ValueError(f"unsupported hidden width {d_ff}")


def kernel(x, p, positions, norm_mix_g, w_in, conv_w, pool_w, pool_scale, attn_sinks, w_branch, w_out, norm_ffn_g, ffn_w_gu, ffn_w_down, moe_router, moe_w_gu, moe_w_down, norm_ple_g, ple_w_proj, ple_w_gate, final_norm_g):
    batch, seq, d = x.shape
    depth = w_in.shape[0]
    n = batch * seq
    tm_in, tm, tile, tm_row, tmg = _tiles(n, seq)
    assert seq % tile == 0 and tile % WINDOW == 0 and n % tm == 0 and n % tm_in == 0 and n % tm_row == 0
    stack_rows = lambda a: a.reshape(a.shape[0], 1, a.shape[1])

    w_in, pool_w, w_branch, w_out = (a.astype(BF16) for a in (w_in, pool_w, w_branch, w_out))
    ffn_w_gu, ffn_w_down, moe_w_gu, moe_w_down = (a.astype(BF16) for a in (ffn_w_gu, ffn_w_down, moe_w_gu, moe_w_down))
    ple_w_gate, ple_w_proj = ple_w_gate.astype(BF16), ple_w_proj.astype(BF16)
    g_mix, g_ffn, g_ple = stack_rows(norm_mix_g), stack_rows(norm_ffn_g), stack_rows(norm_ple_g)
    pool_scale = stack_rows(pool_scale)
    p = p.reshape(depth, n, -1)

    cos, sin = _rope_tables(positions, tm)
    xs = x.reshape(n, d)
    for i in range(depth):
        conv, pq, kv, gates = _in_proj(xs, g_mix, cos, sin, w_in, i, tm_in)
        xs = _mixer(xs, conv, pq, kv, gates, attn_sinks, conv_w, pool_w, pool_scale, w_branch, w_out, i, batch, tile)
        if i % 2 == 0:
            xs = _ffn(xs, g_ffn, ffn_w_gu, ffn_w_down, i, i // 2, tm, _ff_tile(ffn_w_down.shape[1]))
        else:
            xs = _moe(xs, g_ffn, i, moe_router[i // 2], moe_w_gu, moe_w_down, i // 2, tm_row, tmg)
        xs = _ple(xs, p, g_ple, ple_w_gate, ple_w_proj, final_norm_g.reshape(1, d), i, i == depth - 1, tm)
    return xs.reshape(batch, seq, d)
```

```python
import functools

import jax
import jax.numpy as jnp
from jax import lax
from jax.experimental import pallas as pl
from jax.experimental.pallas import tpu as pltpu

F32 = jnp.float32
BF16 = jnp.bfloat16

BRANCH_W = 512
N_BRANCH = 3
POOL_WINDOWS = (2, 4, 8, 16)
POOL_GROUP_W = 128
N_HEADS = 8
N_KV_HEADS = 2
HEADS_PER_KV = N_HEADS // N_KV_HEADS
HEAD_DIM = 64
HALF = HEAD_DIM // 2
KV_W = N_KV_HEADS * HEAD_DIM
WINDOW = 128
ROPE_THETA = 10000.0
NEG_INF = -1e30
EPS = 1e-6
OFF_POOL = 3 * BRANCH_W
OFF_Q = OFF_POOL + BRANCH_W
OFF_K = OFF_Q + N_HEADS * HEAD_DIM
OFF_GATE = OFF_K + 2 * KV_W

LANES = 128
V7X_VMEM_LIMIT_BYTES = 56 * 1024 * 1024
HALO = 16


def _params(*sem):
    return pltpu.CompilerParams(dimension_semantics=sem, vmem_limit_bytes=V7X_VMEM_LIMIT_BYTES)


def _rms(x, g):
    ms = jnp.mean(x * x, axis=-1, keepdims=True)
    return x * lax.rsqrt(ms + EPS) * g


def _sigmoid(x):
    return 0.5 * jnp.tanh(0.5 * x) + 0.5


def _dot(a, b):
    return jnp.dot(a, b, preferred_element_type=F32)


def _layer_row(stacked, layer):
    return pl.BlockSpec((None, 1, stacked.shape[2]), lambda *_: (layer, 0, 0))


def _layer_full(stacked, layer):
    shape = stacked.shape[1:]
    return pl.BlockSpec((None,) + shape, lambda *_: (layer,) + (0,) * len(shape))


def _rope_kernel(pos_ref, freq_ref, sign_ref, cos_ref, sin_ref):
    ang = pos_ref[...].astype(F32) * freq_ref[...]
    cos_ref[...] = jnp.cos(ang)
    sin_ref[...] = jnp.sin(ang) * sign_ref[...]


def _rope_tables(positions, tm):
    n = positions.size
    inv_freq = ROPE_THETA ** (-2.0 * jnp.arange(HALF, dtype=F32) / HEAD_DIM)
    freq = jnp.tile(inv_freq, LANES // HALF).reshape(1, LANES)
    sign = jnp.tile(jnp.concatenate([-jnp.ones(HALF, F32), jnp.ones(HALF, F32)]), LANES // HEAD_DIM).reshape(1, LANES)
    row = pl.BlockSpec((1, LANES), lambda i: (0, 0))
    tab = pl.BlockSpec((tm, LANES), lambda i: (i, 0))
    return pl.pallas_call(
        _rope_kernel,
        grid=(n // tm,),
        in_specs=[pl.BlockSpec((tm, 1), lambda i: (i, 0)), row, row],
        out_specs=[tab, tab],
        out_shape=[jax.ShapeDtypeStruct((n, LANES), F32)] * 2,
        compiler_params=_params("parallel"),
        name="rope_tables",
    )(positions.reshape(n, 1), freq, sign)


def _rotate_half(x, cos, sin_signed):
    width = x.shape[-1]
    lane = lax.broadcasted_iota(jnp.int32, x.shape, 1)
    first = (lane % HEAD_DIM) < HALF
    partner = jnp.where(first, pltpu.roll(x, width - HALF, 1), pltpu.roll(x, HALF, 1))
    return x * cos + partner * sin_signed


def _in_proj_kernel(x_ref, g_ref, cos_ref, sin_ref, w_ref, conv_ref, pq_ref, kv_ref, gate_ref):
    h = _rms(x_ref[...], g_ref[...]).astype(BF16)

    def proj(c0, width):
        return _dot(h, w_ref[:, c0:c0 + width])

    conv_ref[:, :BRANCH_W] = proj(0, BRANCH_W).astype(BF16)
    conv_ref[:, BRANCH_W:] = (proj(BRANCH_W, BRANCH_W) * proj(2 * BRANCH_W, BRANCH_W)).astype(BF16)
    pq_ref[:, :BRANCH_W] = proj(OFF_POOL, BRANCH_W).astype(BF16)
    cos = cos_ref[...]
    sin = sin_ref[...]
    rep = N_HEADS * HEAD_DIM // LANES
    q = _rotate_half(proj(OFF_Q, N_HEADS * HEAD_DIM), jnp.concatenate([cos] * rep, axis=1),
                     jnp.concatenate([sin] * rep, axis=1))
    pq_ref[:, BRANCH_W:] = (q * HEAD_DIM ** -0.5).astype(BF16)
    kv = proj(OFF_K, 2 * KV_W)
    kv_ref[:, :KV_W] = _rotate_half(kv[:, :KV_W], cos, sin).astype(BF16)
    kv_ref[:, KV_W:] = kv[:, KV_W:].astype(BF16)
    d = gate_ref.shape[1] // N_BRANCH
    for r in range(N_BRANCH):
        gate_ref[:, r * d:(r + 1) * d] = _sigmoid(proj(OFF_GATE + r * d, d)).astype(BF16)


def _in_proj(x, g, cos, sin, w_in, layer, tm):
    n, d = x.shape
    d_in = w_in.shape[2]
    tok = lambda width: pl.BlockSpec((tm, width), lambda i: (i, 0))
    widths = (2 * BRANCH_W, OFF_Q + N_HEADS * HEAD_DIM - OFF_POOL, 2 * KV_W, N_BRANCH * d)
    return pl.pallas_call(
        _in_proj_kernel,
        grid=(n // tm,),
        in_specs=[tok(d), _layer_row(g, layer), tok(LANES), tok(LANES),
                  pl.BlockSpec((None, d, d_in), lambda i: (layer, 0, 0), pipeline_mode=pl.Buffered(1))],
        out_specs=[tok(w) for w in widths],
        out_shape=[jax.ShapeDtypeStruct((n, w), BF16) for w in widths],
        compiler_params=_params("parallel"),
        name="in_proj",
    )(x, g, cos, sin, w_in)


def _block_diag_heads(t, lo, use_rolled):
    rolled = pltpu.roll(t, HEAD_DIM, 1)
    low = jnp.where(lo, rolled if use_rolled else t, 0.0).astype(BF16)
    high = jnp.where(lo, 0.0, t if use_rolled else rolled).astype(BF16)
    z = jnp.zeros_like(low)
    return jnp.concatenate([
        jnp.concatenate([low, z], axis=1), jnp.concatenate([high, z], axis=1),
        jnp.concatenate([z, low], axis=1), jnp.concatenate([z, high], axis=1)], axis=0)


def _mixer_kernel(sinks_ref, conv_ref, convh_ref, pq_ref, poolh_ref, kv_ref, kvh_ref, gate_ref, x_ref,
                  convw_ref, poolw_ref, pscale_ref, wbr_ref, wout_ref, out_ref, att_ref, *, tile, layer):
    seq_tile = pl.program_id(1)
    not_first = seq_tile > 0
    w = BRANCH_W

    v = conv_ref[:, w:].astype(F32)
    vh = convh_ref[:, w:].astype(F32)
    ext = jnp.concatenate([jnp.where(not_first, vh, 0.0), v], axis=0)
    cw = convw_ref[...]
    y = cw[2:3] * v + cw[1:2] * pltpu.roll(ext, 1, 0)[HALO:] + cw[0:1] * pltpu.roll(ext, 2, 0)[HALO:]
    y_conv = (conv_ref[:, :w].astype(F32) * y).astype(BF16)

    pu = pq_ref[:, :w].astype(F32)
    s = jnp.concatenate([jnp.where(not_first, poolh_ref[...].astype(F32), 0.0), pu], axis=0)
    pos1 = seq_tile * tile + lax.broadcasted_iota(jnp.int32, (tile, 1), 0) + 1
    mixed = []
    for gi, win in enumerate(POOL_WINDOWS):
        s = s[:, (POOL_GROUP_W if gi else 0):]
        s = s + pltpu.roll(s, win // 2, 0)
        count = jnp.minimum(pos1, win).astype(F32)
        lo_c = gi * POOL_GROUP_W
        pooled = s[HALO:, :POOL_GROUP_W] / count - pu[:, lo_c:lo_c + POOL_GROUP_W]
        mixed.append(_dot(pooled.astype(BF16), poolw_ref[gi]))
    y_pool = (jnp.concatenate(mixed, axis=1) * pscale_ref[...]).astype(BF16)

    kvh = jnp.where(not_first, kvh_ref[...], jnp.zeros_like(kvh_ref[...]))
    kext = jnp.concatenate([kvh, kv_ref[...]], axis=0)
    qw = HEADS_PER_KV * HEAD_DIM
    rows = HEADS_PER_KV * WINDOW
    lo = lax.broadcasted_iota(jnp.int32, (2 * WINDOW, KV_W), 1) < HEAD_DIM
    lane_head = lax.broadcasted_iota(jnp.int32, (WINDOW, qw), 1) // HEAD_DIM
    qi = lax.broadcasted_iota(jnp.int32, (rows, 2 * WINDOW), 0) % WINDOW
    key = lax.broadcasted_iota(jnp.int32, (rows, 2 * WINDOW), 1)
    dist = qi + WINDOW - key
    in_band = (dist >= 0) & (dist < WINDOW)
    row_head = lax.broadcasted_iota(jnp.int32, (rows, 1), 0) // WINDOW
    for j in range(tile // WINDOW):
        band = kext[j * WINDOW:(j + 2) * WINDOW].astype(F32)
        halves = [(t, pltpu.roll(t, HEAD_DIM, 1)) for t in (band[:, :KV_W], band[:, KV_W:])]
        valid = in_band if j else in_band & ((key >= WINDOW) | not_first)
        for g in range(N_KV_HEADS):
            k_rep, v_rep = (
                jnp.concatenate([jnp.where(lo, r, t) if g else jnp.where(lo, t, r)] * (qw // KV_W), axis=1).astype(BF16)
                for t, r in halves)
            qg = pq_ref[j * WINDOW:(j + 1) * WINDOW, w + g * qw:w + (g + 1) * qw]
            q_heads = jnp.concatenate(
                [jnp.where(lane_head == hh, qg, jnp.zeros_like(qg)) for hh in range(HEADS_PER_KV)], axis=0)
            scores = lax.dot_general(q_heads, k_rep, (((1,), (1,)), ((), ())), preferred_element_type=F32)
            sink = jnp.zeros((rows, 1), F32)
            for hh in range(HEADS_PER_KV):
                sink = jnp.where(row_head == hh, sinks_ref[layer, g * HEADS_PER_KV + hh], sink)
            sh = jnp.where(valid, scores, NEG_INF)
            m = jnp.maximum(jnp.max(sh, axis=-1, keepdims=True), sink)
            pexp = jnp.exp(sh - m)
            denom = jnp.sum(pexp, axis=-1, keepdims=True) + jnp.exp(sink - m)
            res = _dot((pexp * (1.0 / denom)).astype(BF16), v_rep)
            out = res[:WINDOW]
            for hh in range(1, HEADS_PER_KV):
                out = jnp.where(lane_head == hh, res[hh * WINDOW:(hh + 1) * WINDOW], out)
            att_ref[j * WINDOW:(j + 1) * WINDOW, g * qw:(g + 1) * qw] = out.astype(BF16)

    d = x_ref.shape[1]
    merged = None
    for r, br in enumerate((y_conv, y_pool, att_ref[...])):
        term = gate_ref[:, r * d:(r + 1) * d].astype(F32) * _dot(br, wbr_ref[r])
        merged = term if merged is None else merged + term
    out_ref[...] = x_ref[...] + _dot(merged.astype(BF16), wout_ref[...])


def _mixer(x, conv, pq, kv, gates, sinks, conv_w, pool_w, pool_scale, w_branch, w_out, layer, batch, tile):
    n, d = x.shape
    n_t = n // batch // tile
    t_idx = lambda b, i: b * n_t + i
    tok = lambda width: pl.BlockSpec((tile, width), lambda b, i: (t_idx(b, i), 0))
    prev = lambda rows, width: pl.BlockSpec(
        (rows, width), lambda b, i: (jnp.maximum(t_idx(b, i) * (tile // rows) - 1, 0), 0))
    return pl.pallas_call(
        functools.partial(_mixer_kernel, tile=tile, layer=layer),
        grid=(batch, n_t),
        in_specs=[pl.BlockSpec(memory_space=pltpu.SMEM),
                  tok(conv.shape[1]), prev(HALO, conv.shape[1]),
                  tok(pq.shape[1]), prev(HALO, BRANCH_W),
                  tok(kv.shape[1]), prev(WINDOW, kv.shape[1]),
                  tok(gates.shape[1]), tok(d),
                  _layer_full(conv_w, layer), _layer_full(pool_w, layer), _layer_row(pool_scale, layer),
                  _layer_full(w_branch, layer), _layer_full(w_out, layer)],
        out_specs=tok(d),
        out_shape=jax.ShapeDtypeStruct((n, d), F32),
        scratch_shapes=[pltpu.VMEM((tile, N_HEADS * HEAD_DIM), BF16)],
        compiler_params=_params("parallel", "parallel"),
        name="mixer",
    )(sinks, conv, conv, pq, pq, kv, kv, gates, x, conv_w, pool_w, pool_scale, w_branch, w_out)


def _ffn_kernel(x_ref, g_ref, wg_ref, wu_ref, wd_ref, out_ref, h_ref, acc_ref):
    j = pl.program_id(1)

    @pl.when(j == 0)
    def _():
        h_ref[...] = _rms(x_ref[...], g_ref[...]).astype(BF16)
        acc_ref[...] = jnp.zeros_like(acc_ref)

    h = h_ref[...]
    hg = _dot(h, wg_ref[...])
    act = (hg * _sigmoid(hg) * _dot(h, wu_ref[...])).astype(BF16)
    acc_ref[...] += _dot(act, wd_ref[...])

    @pl.when(j == pl.num_programs(1) - 1)
    def _():
        out_ref[...] = x_ref[...] + acc_ref[...]


def _ffn(x, g, w_gu, w_down, layer, idx, tm, tf):
    n, d = x.shape
    n_f = w_down.shape[1] // tf
    return pl.pallas_call(
        _ffn_kernel,
        grid=(n // tm, n_f),
        in_specs=[pl.BlockSpec((tm, d), lambda i, j: (i, 0)),
                  _layer_row(g, layer),
                  pl.BlockSpec((None, d, tf), lambda i, j: (idx, 0, j)),
                  pl.BlockSpec((None, d, tf), lambda i, j: (idx, 0, j + n_f)),
                  pl.BlockSpec((None, tf, d), lambda i, j: (idx, j, 0))],
        out_specs=pl.BlockSpec((tm, d), lambda i, j: (i, 0)),
        out_shape=jax.ShapeDtypeStruct((n, d), F32),
        scratch_shapes=[pltpu.VMEM((tm, d), BF16), pltpu.VMEM((tm, d), F32)],
        compiler_params=_params("parallel", "arbitrary"),
        name="ffn",
    )(x, g, w_gu, w_gu, w_down)


SEG_ALIGN = 8
R_LOCAL, R_WEIGHT = 2, 4
SEG_UNITS, SEG_LOCAL, SEG_BASE = 0, 8, 16


def _local_rows(tm, n_experts):
    return -(-(2 * tm + SEG_ALIGN * n_experts) // LANES) * LANES


def _router_kernel(x_ref, g_ref, wr_ref, route_ref, seg_ref, count_ref, base_ref, *, n_experts):
    @pl.when(pl.program_id(0) == 0)
    def _():
        base_ref[...] = jnp.zeros_like(base_ref)

    tm = x_ref.shape[0]
    h = _rms(x_ref[...], g_ref[...]).astype(BF16)
    lane_i = lax.broadcasted_iota(jnp.int32, (tm, LANES), 1)
    lane = lane_i.astype(F32)
    logits = jnp.where(lane_i < n_experts, _dot(h, wr_ref[...]), NEG_INF)

    def take_max(vals):
        top = jnp.max(vals, axis=-1, keepdims=True)
        idx = jnp.min(jnp.where(vals == top, lane, float(LANES)), axis=-1, keepdims=True)
        return top, idx

    v1, i1 = take_max(logits)
    v2, i2 = take_max(jnp.where(lane == i1, NEG_INF, logits))
    e2 = jnp.exp(v2 - v1)
    w1 = 1.0 / (1.0 + e2)
    hit1, hit2 = lane == i1, lane == i2
    onehot = jnp.where(hit1 | hit2, 1.0, 0.0)
    earlier = (lax.broadcasted_iota(jnp.int32, (tm, tm), 0) > lax.broadcasted_iota(jnp.int32, (tm, tm), 1))
    before = _dot(earlier.astype(BF16), onehot.astype(BF16))
    units = jnp.ceil(jnp.sum(onehot, axis=0, keepdims=True) * (1.0 / SEG_ALIGN))
    lower = (lax.broadcasted_iota(jnp.int32, (LANES, LANES), 0) < lax.broadcasted_iota(jnp.int32, (LANES, LANES), 1))
    local = SEG_ALIGN * _dot(jnp.broadcast_to(units, (SEG_ALIGN, LANES)).astype(BF16), lower.astype(BF16))[:1]
    row = before + local
    l1 = jnp.sum(jnp.where(hit1, row, 0.0), axis=-1, keepdims=True)
    l2 = jnp.sum(jnp.where(hit2, row, 0.0), axis=-1, keepdims=True)
    route = jnp.zeros((tm, LANES), F32)
    for k, val in enumerate((i1, i2, l1, l2, w1, e2 * w1)):
        route = jnp.where(lane_i == k, val, route)
    route_ref[...] = route
    roll_to = lambda a, k: pltpu.roll(a, k, 1)
    first = lane_i[:1] < n_experts
    seg = (jnp.where(first, units, 0.0) + roll_to(jnp.where(first, local, 0.0), SEG_LOCAL)
           + roll_to(jnp.where(first, base_ref[...], 0.0), SEG_BASE))
    seg_ref[...] = seg
    base_ref[...] += SEG_ALIGN * units
    count_ref[...] = base_ref[...]


def _router(x, g, w_router, layer, tm):
    n, d = x.shape
    n_experts = w_router.shape[1]
    assert n_experts <= SEG_LOCAL
    wr = jnp.pad(w_router, ((0, 0), (0, LANES - n_experts))).astype(BF16)
    return pl.pallas_call(
        functools.partial(_router_kernel, n_experts=n_experts),
        grid=(n // tm,),
        in_specs=[pl.BlockSpec((tm, d), lambda i: (i, 0)), _layer_row(g, layer),
                  pl.BlockSpec((d, LANES), lambda i: (0, 0))],
        out_specs=[pl.BlockSpec((tm, LANES), lambda i: (i, 0)),
                   pl.BlockSpec((None, 1, LANES), lambda i: (i, 0, 0)),
                   pl.BlockSpec((1, LANES), lambda i: (0, 0))],
        out_shape=[jax.ShapeDtypeStruct((n, LANES), F32), jax.ShapeDtypeStruct((n // tm, 1, LANES), F32),
                   jax.ShapeDtypeStruct((1, LANES), F32)],
        scratch_shapes=[pltpu.VMEM((1, LANES), F32)],
        compiler_params=_params("arbitrary"),
        name="router",
    )(x, g, wr)


def _for_each_chunk(seg_ref, n_experts, max_units, fn):
    for e in range(n_experts):
        units = seg_ref[0, 0, SEG_UNITS + e]
        local = seg_ref[0, 0, SEG_LOCAL + e]
        grouped = seg_ref[0, 0, SEG_BASE + e]
        for b in range(max_units.bit_length()):
            skip = (units >> (b + 1) << (b + 1)) * SEG_ALIGN

            @pl.when((units >> b) & 1 == 1)
            def _():
                fn(pl.multiple_of(local + skip, SEG_ALIGN), pl.multiple_of(grouped + skip, SEG_ALIGN), SEG_ALIGN << b)


def _dispatch_kernel(pad_ref, seg_ref, prev_seg_ref, x_ref, g_ref, route_ref, xs_ref, sorted_ref, zero_ref, sem,
                     *, n_experts, n_tokens):
    step = pl.program_id(0)
    tm = x_ref.shape[0]
    tmg = zero_ref.shape[0]
    max_tail = (xs_ref.shape[0] - 2 * n_tokens) // tmg
    fill = lambda row: pltpu.make_async_copy(zero_ref, xs_ref.at[pl.ds(pl.multiple_of(row, SEG_ALIGN), tmg)], sem.at[0])

    @pl.when(step == 0)
    def _():
        zero_ref[...] = jnp.zeros_like(zero_ref)
        for e in range(n_experts):
            fill(pad_ref[e]).start()
        for e in range(n_experts):
            fill(pad_ref[e]).wait()
        for t in range(max_tail):
            row = pad_ref[n_experts] + t * tmg

            @pl.when(row < xs_ref.shape[0])
            def _():
                fill(row).start()
                fill(row).wait()

    slot = step % 2
    n_local = sorted_ref.shape[1]
    h = _rms(x_ref[...], g_ref[...]).astype(BF16)
    owner = route_ref[...].T
    row = lax.broadcasted_iota(jnp.int32, (n_local, tm), 0).astype(F32)
    pick = (row == owner[R_LOCAL:R_LOCAL + 1]) | (row == owner[R_LOCAL + 1:R_LOCAL + 2])
    sorted_ref[slot] = _dot(jnp.where(pick, 1.0, 0.0).astype(BF16), h)

    def segment_copy(buf_slot):
        return lambda local, grouped, rows: pltpu.make_async_copy(
            sorted_ref.at[buf_slot, pl.ds(local, rows)], xs_ref.at[pl.ds(grouped, rows)], sem.at[buf_slot])

    _for_each_chunk(seg_ref, n_experts, tm // SEG_ALIGN, lambda *a: segment_copy(slot)(*a).start())

    @pl.when(step > 0)
    def _():
        _for_each_chunk(prev_seg_ref, n_experts, tm // SEG_ALIGN, lambda *a: segment_copy(1 - slot)(*a).wait())

    @pl.when(step == pl.num_programs(0) - 1)
    def _():
        _for_each_chunk(seg_ref, n_experts, tm // SEG_ALIGN, lambda *a: segment_copy(slot)(*a).wait())


def _dispatch(x, g, layer, route, seg, pad_start, rows, tm, tmg):
    n, d = x.shape
    n_experts = pad_start.shape[0] - 1
    seg_spec = lambda shift: pl.BlockSpec((1, 1, seg.shape[2]), lambda i, pad: (jnp.maximum(i - shift, 0), 0, 0),
                                          memory_space=pltpu.SMEM)
    return pl.pallas_call(
        functools.partial(_dispatch_kernel, n_experts=n_experts, n_tokens=n),
        grid_spec=pltpu.PrefetchScalarGridSpec(
            num_scalar_prefetch=1,
            grid=(n // tm,),
            in_specs=[seg_spec(0), seg_spec(1),
                      pl.BlockSpec((tm, d), lambda i, pad: (i, 0)),
                      _layer_row(g, layer),
                      pl.BlockSpec((tm, LANES), lambda i, pad: (i, 0))],
            out_specs=pl.BlockSpec(memory_space=pl.ANY),
            scratch_shapes=[pltpu.VMEM((2, _local_rows(tm, n_experts), d), F32), pltpu.VMEM((tmg, d), F32),
                            pltpu.SemaphoreType.DMA((2,))],
        ),
        out_shape=jax.ShapeDtypeStruct((rows + tmg, d), F32),
        compiler_params=_params("arbitrary"),
        name="dispatch",
    )(pad_start, seg, seg, x, g, route)


def _expert_kernel(te_ref, used_ref, xs_ref, wg_ref, wu_ref, wd_ref, y_ref):
    @pl.when(pl.program_id(0) < used_ref[0])
    def _():
        h = xs_ref[...].astype(BF16)
        hg = _dot(h, wg_ref[0])
        act = (hg * _sigmoid(hg) * _dot(h, wu_ref[0])).astype(BF16)
        y_ref[...] = _dot(act, wd_ref[0])

    @pl.when(pl.program_id(0) >= used_ref[0])
    def _():
        y_ref[...] = jnp.zeros_like(y_ref)


def _experts(xs, tile_expert, n_used, w_gu, w_down, idx, rows, tmg):
    d = xs.shape[1]
    d_ff = w_down.shape[2]
    live = lambda t, te, used: jnp.minimum(t, used[0] - 1)
    return pl.pallas_call(
        _expert_kernel,
        grid_spec=pltpu.PrefetchScalarGridSpec(
            num_scalar_prefetch=2,
            grid=(rows // tmg,),
            in_specs=[pl.BlockSpec((tmg, d), lambda t, te, used: (live(t, te, used), 0)),
                      pl.BlockSpec((None, 1, d, d_ff), lambda t, te, used: (idx, te[live(t, te, used)], 0, 0)),
                      pl.BlockSpec((None, 1, d, d_ff), lambda t, te, used: (idx, te[live(t, te, used)], 0, 1)),
                      pl.BlockSpec((None, 1, d_ff, d), lambda t, te, used: (idx, te[live(t, te, used)], 0, 0))],
            out_specs=pl.BlockSpec((tmg, d), lambda t, te, used: (t, 0)),
        ),
        out_shape=jax.ShapeDtypeStruct((rows, d), F32),
        compiler_params=_params("arbitrary"),
        name="experts",
    )(tile_expert, n_used, xs, w_gu, w_gu, w_down)


def _combine_kernel(seg_ref, next_seg_ref, x_ref, route_ref, y_ref, out_ref, buf_ref, sem, *, n_experts):
    step = pl.program_id(0)
    tm = x_ref.shape[0]
    slot = step % 2
    n_local = buf_ref.shape[1]

    def segment_copy(buf_slot):
        return lambda local, grouped, rows: pltpu.make_async_copy(
            y_ref.at[pl.ds(grouped, rows)], buf_ref.at[buf_slot, pl.ds(local, rows)], sem.at[buf_slot])

    @pl.when(step == 0)
    def _():
        buf_ref[...] = jnp.zeros_like(buf_ref)
        _for_each_chunk(seg_ref, n_experts, tm // SEG_ALIGN, lambda *a: segment_copy(slot)(*a).start())

    @pl.when(step < pl.num_programs(0) - 1)
    def _():
        _for_each_chunk(next_seg_ref, n_experts, tm // SEG_ALIGN, lambda *a: segment_copy(1 - slot)(*a).start())

    _for_each_chunk(seg_ref, n_experts, tm // SEG_ALIGN, lambda *a: segment_copy(slot)(*a).wait())

    route = route_ref[...]
    y = buf_ref[slot].astype(BF16)
    col = lax.broadcasted_iota(jnp.int32, (tm, n_local), 1).astype(F32)
    weigh = jnp.zeros((tm, n_local), F32)
    for k in range(2):
        weigh = jnp.where(col == route[:, R_LOCAL + k:R_LOCAL + k + 1], route[:, R_WEIGHT + k:R_WEIGHT + k + 1], weigh)
    out_ref[...] = x_ref[...] + _dot(weigh.astype(BF16), y)


def _combine(x, route, seg, y, tm):
    n, d = x.shape
    n_experts = SEG_LOCAL - SEG_UNITS
    last = n // tm - 1
    seg_spec = lambda shift: pl.BlockSpec((1, 1, seg.shape[2]), lambda i: (jnp.minimum(i + shift, last), 0, 0),
                                          memory_space=pltpu.SMEM)
    return pl.pallas_call(
        functools.partial(_combine_kernel, n_experts=n_experts),
        grid=(n // tm,),
        in_specs=[seg_spec(0), seg_spec(1),
                  pl.BlockSpec((tm, d), lambda i: (i, 0)),
                  pl.BlockSpec((tm, LANES), lambda i: (i, 0)),
                  pl.BlockSpec(memory_space=pl.ANY)],
        out_specs=pl.BlockSpec((tm, d), lambda i: (i, 0)),
        out_shape=jax.ShapeDtypeStruct((n, d), F32),
        scratch_shapes=[pltpu.VMEM((2, _local_rows(tm, n_experts), d), F32), pltpu.SemaphoreType.DMA((2,))],
        compiler_params=_params("arbitrary"),
        name="combine",
    )(seg, seg, x, route, y)


def _moe(x, g, layer, w_router, w_gu, w_down, idx, tm, tmg):
    n = x.shape[0]
    n_experts = w_router.shape[1]
    route, seg, count = _router(x, g, w_router, layer, tm)
    counts = count[0, :n_experts].astype(jnp.int32)
    padded = (counts + tmg - 1) // tmg * tmg
    ends = jnp.cumsum(padded)
    starts = ends - padded
    seg = seg.astype(jnp.int32)[:, :, :4 * SEG_LOCAL]
    seg = seg.at[:, 0, SEG_BASE:SEG_BASE + n_experts].add(starts)
    rows = -(-(2 * n + (n // tm) * n_experts * SEG_ALIGN) // tmg) * tmg + n_experts * tmg
    n_used = (ends[-1:] // tmg).astype(jnp.int32)
    tile_row = jnp.arange(rows // tmg, dtype=jnp.int32) * tmg
    tile_expert = jnp.minimum(jnp.sum(tile_row[:, None] >= ends[None, :], axis=-1), n_experts - 1).astype(jnp.int32)
    pad_start = jnp.concatenate([starts + counts, ends[-1:]]).astype(jnp.int32)

    xs = _dispatch(x, g, layer, route, seg, pad_start, rows, tm, tmg)
    y = _experts(xs, tile_expert, n_used, w_gu, w_down, idx, rows, tmg)
    return _combine(x, route, seg, y, tm)


def _ple_kernel(x_ref, p_ref, g_ref, wg_ref, wp_ref, gf_ref, out_ref, *, final):
    x = x_ref[...]
    h = _rms(x, g_ref[...]).astype(BF16)
    y = x + _sigmoid(_dot(h, wg_ref[...])) * _dot(p_ref[...].astype(BF16), wp_ref[...])
    out_ref[...] = _rms(y, gf_ref[...]) if final else y


def _ple(x, p, g, w_gate, w_proj, g_final, layer, final, tm):
    n, d = x.shape
    dp = p.shape[2]
    return pl.pallas_call(
        functools.partial(_ple_kernel, final=final),
        grid=(n // tm,),
        in_specs=[pl.BlockSpec((tm, d), lambda i: (i, 0)), pl.BlockSpec((None, tm, dp), lambda i: (layer, i, 0)),
                  _layer_row(g, layer), _layer_full(w_gate, layer), _layer_full(w_proj, layer),
                  pl.BlockSpec((1, d), lambda i: (0, 0))],
        out_specs=pl.BlockSpec((tm, d), lambda i: (i, 0)),
        out_shape=jax.ShapeDtypeStruct((n, d), F32),
        compiler_params=_params("parallel"),
        name="ple",
    )(x, p, g, w_gate, w_proj, g_final)


def _tiles(n, seq):
    return min(512, n), min(1024, n), min(512, seq), min(512, n), 512


def _ff_tile(d_ff):
    for tf in (512, 896, 256, 128):
        if d_ff % tf == 0:
            return tf
    raise ValueError(f"unsupported hidden width {d_ff}")


def kernel(x, p, positions, norm_mix_g, w_in, conv_w, pool_w, pool_scale, attn_sinks, w_branch, w_out, norm_ffn_g, ffn_w_gu, ffn_w_down, moe_router, moe_w_gu, moe_w_down, norm_ple_g, ple_w_proj, ple_w_gate, final_norm_g):
    batch, seq, d = x.shape
    depth = w_in.shape[0]
    n = batch * seq
    tm_in, tm, tile, tm_row, tmg = _tiles(n, seq)
    assert seq % tile == 0 and tile % WINDOW == 0 and n % tm == 0 and n % tm_in == 0 and n % tm_row == 0
    stack_rows = lambda a: a.reshape(a.shape[0], 1, a.shape[1])

    w_in, pool_w, w_branch, w_out = (a.astype(BF16) for a in (w_in, pool_w, w_branch, w_out))
    ffn_w_gu, ffn_w_down, moe_w_gu, moe_w_down = (a.astype(BF16) for a in (ffn_w_gu, ffn_w_down, moe_w_gu, moe_w_down))
    ple_w_gate, ple_w_proj = ple_w_gate.astype(BF16), ple_w_proj.astype(BF16)
    g_mix, g_ffn, g_ple = stack_rows(norm_mix_g), stack_rows(norm_ffn_g), stack_rows(norm_ple_g)
    pool_scale = stack_rows(pool_scale)
    p = p.reshape(depth, n, -1)

    cos, sin = _rope_tables(positions, tm)
    xs = x.reshape(n, d)
    for i in range(depth):
        conv, pq, kv, gates = _in_proj(xs, g_mix, cos, sin, w_in, i, tm_in)
        xs = _mixer(xs, conv, pq, kv, gates, attn_sinks, conv_w, pool_w, pool_scale, w_branch, w_out, i, batch, tile)
        if i % 2 == 0:
            xs = _ffn(xs, g_ffn, ffn_w_gu, ffn_w_down, i, i // 2, tm, _ff_tile(ffn_w_down.shape[1]))
        else:
            xs = _moe(xs, g_ffn, i, moe_router[i // 2], moe_w_gu, moe_w_down, i // 2, tm_row, tmg)
        xs = _ple(xs, p, g_ple, ple_w_gate, ple_w_proj, final_norm_g.reshape(1, d), i, i == depth - 1, tm)
    return xs.reshape(batch, seq, d)
```

```python
import functools

import jax
import jax.numpy as jnp
from jax import lax
from jax.experimental import pallas as pl
from jax.experimental.pallas import tpu as pltpu

F32 = jnp.float32
BF16 = jnp.bfloat16

BRANCH_W = 512
N_BRANCH = 3
POOL_WINDOWS = (2, 4, 8, 16)
POOL_GROUP_W = 128
N_HEADS = 8
N_KV_HEADS = 2
HEADS_PER_KV = N_HEADS // N_KV_HEADS
HEAD_DIM = 64
HALF = HEAD_DIM // 2
KV_W = N_KV_HEADS * HEAD_DIM
WINDOW = 128
ROPE_THETA = 10000.0
NEG_INF = -1e30
EPS = 1e-6
OFF_POOL = 3 * BRANCH_W
OFF_Q = OFF_POOL + BRANCH_W
OFF_K = OFF_Q + N_HEADS * HEAD_DIM
OFF_GATE = OFF_K + 2 * KV_W

LANES = 128
V7X_VMEM_LIMIT_BYTES = 56 * 1024 * 1024
HALO = 16


def _params(*sem):
    return pltpu.CompilerParams(dimension_semantics=sem, vmem_limit_bytes=V7X_VMEM_LIMIT_BYTES)


def _rms(x, g):
    ms = jnp.mean(x * x, axis=-1, keepdims=True)
    return x * lax.rsqrt(ms + EPS) * g


def _sigmoid(x):
    return 0.5 * jnp.tanh(0.5 * x) + 0.5


def _dot(a, b):
    return jnp.dot(a, b, preferred_element_type=F32)


def _ple_tail(x, p_ref, g_ref, wg_ref, wp_ref, gf_ref, final):
    h = _rms(x, g_ref[...]).astype(BF16)
    y = x + _sigmoid(_dot(h, wg_ref[...])) * _dot(p_ref[...].astype(BF16), wp_ref[...])
    return _rms(y, gf_ref[...]) if final else y


def _ple_specs(ple, layer, tm):
    p, g, w_gate, w_proj, g_final = ple
    return [pl.BlockSpec((None, tm, p.shape[2]), lambda i, *_: (layer, i, 0)), _layer_row(g, layer),
            _layer_full(w_gate, layer), _layer_full(w_proj, layer),
            pl.BlockSpec(g_final.shape, lambda *_: (0, 0))]


def _layer_row(stacked, layer):
    return pl.BlockSpec((None, 1, stacked.shape[2]), lambda *_: (layer, 0, 0))


def _layer_full(stacked, layer):
    shape = stacked.shape[1:]
    return pl.BlockSpec((None,) + shape, lambda *_: (layer,) + (0,) * len(shape))


def _rope_kernel(pos_ref, freq_ref, sign_ref, cos_ref, sin_ref):
    ang = pos_ref[...].astype(F32) * freq_ref[...]
    cos_ref[...] = jnp.cos(ang)
    sin_ref[...] = jnp.sin(ang) * sign_ref[...]


def _rope_tables(positions, tm):
    n = positions.size
    inv_freq = ROPE_THETA ** (-2.0 * jnp.arange(HALF, dtype=F32) / HEAD_DIM)
    freq = jnp.tile(inv_freq, LANES // HALF).reshape(1, LANES)
    sign = jnp.tile(jnp.concatenate([-jnp.ones(HALF, F32), jnp.ones(HALF, F32)]), LANES // HEAD_DIM).reshape(1, LANES)
    row = pl.BlockSpec((1, LANES), lambda i: (0, 0))
    tab = pl.BlockSpec((tm, LANES), lambda i: (i, 0))
    return pl.pallas_call(
        _rope_kernel,
        grid=(n // tm,),
        in_specs=[pl.BlockSpec((tm, 1), lambda i: (i, 0)), row, row],
        out_specs=[tab, tab],
        out_shape=[jax.ShapeDtypeStruct((n, LANES), F32)] * 2,
        compiler_params=_params("parallel"),
        name="rope_tables",
    )(positions.reshape(n, 1), freq, sign)


def _rotate_half(x, cos, sin_signed):
    width = x.shape[-1]
    lane = lax.broadcasted_iota(jnp.int32, x.shape, 1)
    first = (lane % HEAD_DIM) < HALF
    partner = jnp.where(first, pltpu.roll(x, width - HALF, 1), pltpu.roll(x, HALF, 1))
    return x * cos + partner * sin_signed


def _in_proj_kernel(x_ref, g_ref, cos_ref, sin_ref, w_ref, conv_ref, pq_ref, kv_ref, gate_ref):
    h = _rms(x_ref[...], g_ref[...]).astype(BF16)

    def proj(c0, width):
        return _dot(h, w_ref[:, c0:c0 + width])

    conv_ref[:, :BRANCH_W] = proj(0, BRANCH_W).astype(BF16)
    conv_ref[:, BRANCH_W:] = (proj(BRANCH_W, BRANCH_W) * proj(2 * BRANCH_W, BRANCH_W)).astype(BF16)
    pq_ref[:, :BRANCH_W] = proj(OFF_POOL, BRANCH_W).astype(BF16)
    cos = cos_ref[...]
    sin = sin_ref[...]
    rep = N_HEADS * HEAD_DIM // LANES
    q = _rotate_half(proj(OFF_Q, N_HEADS * HEAD_DIM), jnp.concatenate([cos] * rep, axis=1),
                     jnp.concatenate([sin] * rep, axis=1))
    pq_ref[:, BRANCH_W:] = (q * HEAD_DIM ** -0.5).astype(BF16)
    kv = proj(OFF_K, 2 * KV_W)
    kv_ref[:, :KV_W] = _rotate_half(kv[:, :KV_W], cos, sin).astype(BF16)
    kv_ref[:, KV_W:] = kv[:, KV_W:].astype(BF16)
    d = gate_ref.shape[1] // N_BRANCH
    for r in range(N_BRANCH):
        gate_ref[:, r * d:(r + 1) * d] = _sigmoid(proj(OFF_GATE + r * d, d)).astype(BF16)


def _in_proj(x, g, cos, sin, w_in, layer, tm):
    n, d = x.shape
    d_in = w_in.shape[2]
    tok = lambda width: pl.BlockSpec((tm, width), lambda i: (i, 0))
    widths = (2 * BRANCH_W, OFF_Q + N_HEADS * HEAD_DIM - OFF_POOL, 2 * KV_W, N_BRANCH * d)
    return pl.pallas_call(
        _in_proj_kernel,
        grid=(n // tm,),
        in_specs=[tok(d), _layer_row(g, layer), tok(LANES), tok(LANES),
                  pl.BlockSpec((None, d, d_in), lambda i: (layer, 0, 0), pipeline_mode=pl.Buffered(1))],
        out_specs=[tok(w) for w in widths],
        out_shape=[jax.ShapeDtypeStruct((n, w), BF16) for w in widths],
        compiler_params=_params("parallel"),
        name="in_proj",
    )(x, g, cos, sin, w_in)


def _mixer_kernel(sinks_ref, conv_ref, convh_ref, pq_ref, poolh_ref, kv_ref, kvh_ref, gate_ref, x_ref,
                  convw_ref, poolw_ref, pscale_ref, wbr_ref, wout_ref, out_ref, att_ref, *, tile, layer):
    seq_tile = pl.program_id(1)
    not_first = seq_tile > 0
    w = BRANCH_W

    v = conv_ref[:, w:].astype(F32)
    vh = convh_ref[:, w:].astype(F32)
    ext = jnp.concatenate([jnp.where(not_first, vh, 0.0), v], axis=0)
    cw = convw_ref[...]
    y = cw[2:3] * v + cw[1:2] * pltpu.roll(ext, 1, 0)[HALO:] + cw[0:1] * pltpu.roll(ext, 2, 0)[HALO:]
    y_conv = (conv_ref[:, :w].astype(F32) * y).astype(BF16)

    pu = pq_ref[:, :w].astype(F32)
    s = jnp.concatenate([jnp.where(not_first, poolh_ref[...].astype(F32), 0.0), pu], axis=0)
    pos1 = seq_tile * tile + lax.broadcasted_iota(jnp.int32, (tile, 1), 0) + 1
    mixed = []
    for gi, win in enumerate(POOL_WINDOWS):
        s = s[:, (POOL_GROUP_W if gi else 0):]
        s = s + pltpu.roll(s, win // 2, 0)
        count = jnp.minimum(pos1, win).astype(F32)
        lo_c = gi * POOL_GROUP_W
        pooled = s[HALO:, :POOL_GROUP_W] / count - pu[:, lo_c:lo_c + POOL_GROUP_W]
        mixed.append(_dot(pooled.astype(BF16), poolw_ref[gi]))
    y_pool = (jnp.concatenate(mixed, axis=1) * pscale_ref[...]).astype(BF16)

    d = x_ref.shape[1]
    gated = lambda r, br: gate_ref[:, r * d:(r + 1) * d].astype(F32) * _dot(br, wbr_ref[r])
    merged = gated(0, y_conv) + gated(1, y_pool)

    kvh = jnp.where(not_first, kvh_ref[...], jnp.zeros_like(kvh_ref[...]))
    kext = jnp.concatenate([kvh, kv_ref[...]], axis=0)
    qw = HEADS_PER_KV * HEAD_DIM
    rows = HEADS_PER_KV * WINDOW
    lo = lax.broadcasted_iota(jnp.int32, (2 * WINDOW, KV_W), 1) < HEAD_DIM
    lane_head = lax.broadcasted_iota(jnp.int32, (WINDOW, qw), 1) // HEAD_DIM
    qi = lax.broadcasted_iota(jnp.int32, (rows, 2 * WINDOW), 0) % WINDOW
    key = lax.broadcasted_iota(jnp.int32, (rows, 2 * WINDOW), 1)
    dist = qi + WINDOW - key
    in_band = (dist >= 0) & (dist < WINDOW)
    row_head = lax.broadcasted_iota(jnp.int32, (rows, 1), 0) // WINDOW
    for j in range(tile // WINDOW):
        band = kext[j * WINDOW:(j + 2) * WINDOW].astype(F32)
        halves = [(t, pltpu.roll(t, HEAD_DIM, 1)) for t in (band[:, :KV_W], band[:, KV_W:])]
        valid = in_band if j else in_band & ((key >= WINDOW) | not_first)
        for g in range(N_KV_HEADS):
            k_rep, v_rep = (
                jnp.concatenate([jnp.where(lo, r, t) if g else jnp.where(lo, t, r)] * (qw // KV_W), axis=1).astype(BF16)
                for t, r in halves)
            qg = pq_ref[j * WINDOW:(j + 1) * WINDOW, w + g * qw:w + (g + 1) * qw]
            q_heads = jnp.concatenate(
                [jnp.where(lane_head == hh, qg, jnp.zeros_like(qg)) for hh in range(HEADS_PER_KV)], axis=0)
            scores = lax.dot_general(q_heads, k_rep, (((1,), (1,)), ((), ())), preferred_element_type=F32)
            sink = jnp.zeros((rows, 1), F32)
            for hh in range(HEADS_PER_KV):
                sink = jnp.where(row_head == hh, sinks_ref[layer, g * HEADS_PER_KV + hh], sink)
            sh = jnp.where(valid, scores, NEG_INF)
            m = jnp.maximum(jnp.max(sh, axis=-1, keepdims=True), sink)
            pexp = jnp.exp(sh - m)
            denom = jnp.sum(pexp, axis=-1, keepdims=True) + jnp.exp(sink - m)
            res = _dot((pexp * (1.0 / denom)).astype(BF16), v_rep)
            out = res[:WINDOW]
            for hh in range(1, HEADS_PER_KV):
                out = jnp.where(lane_head == hh, res[hh * WINDOW:(hh + 1) * WINDOW], out)
            att_ref[j * WINDOW:(j + 1) * WINDOW, g * qw:(g + 1) * qw] = out.astype(BF16)

    merged = merged + gated(2, att_ref[...])
    out_ref[...] = x_ref[...] + _dot(merged.astype(BF16), wout_ref[...])


def _mixer(x, conv, pq, kv, gates, sinks, conv_w, pool_w, pool_scale, w_branch, w_out, layer, batch, tile):
    n, d = x.shape
    n_t = n // batch // tile
    t_idx = lambda b, i: b * n_t + i
    tok = lambda width: pl.BlockSpec((tile, width), lambda b, i: (t_idx(b, i), 0))
    prev = lambda rows, width: pl.BlockSpec(
        (rows, width), lambda b, i: (jnp.maximum(t_idx(b, i) * (tile // rows) - 1, 0), 0))
    return pl.pallas_call(
        functools.partial(_mixer_kernel, tile=tile, layer=layer),
        grid=(batch, n_t),
        in_specs=[pl.BlockSpec(memory_space=pltpu.SMEM),
                  tok(conv.shape[1]), prev(HALO, conv.shape[1]),
                  tok(pq.shape[1]), prev(HALO, BRANCH_W),
                  tok(kv.shape[1]), prev(WINDOW, kv.shape[1]),
                  tok(gates.shape[1]), tok(d),
                  _layer_full(conv_w, layer), _layer_full(pool_w, layer), _layer_row(pool_scale, layer),
                  _layer_full(w_branch, layer), _layer_full(w_out, layer)],
        out_specs=tok(d),
        out_shape=jax.ShapeDtypeStruct((n, d), F32),
        scratch_shapes=[pltpu.VMEM((tile, N_HEADS * HEAD_DIM), BF16)],
        compiler_params=_params("parallel", "parallel"),
        name="mixer",
    )(sinks, conv, conv, pq, pq, kv, kv, gates, x, conv_w, pool_w, pool_scale, w_branch, w_out)


def _ffn_kernel(x_ref, g_ref, wg_ref, wu_ref, wd_ref, p_ref, gp_ref, wpg_ref, wpp_ref, gf_ref, out_ref, h_ref, acc_ref,
                *, final):
    j = pl.program_id(1)

    @pl.when(j == 0)
    def _():
        h_ref[...] = _rms(x_ref[...], g_ref[...]).astype(BF16)
        acc_ref[...] = jnp.zeros_like(acc_ref)

    h = h_ref[...]
    hg = _dot(h, wg_ref[...])
    act = (hg * _sigmoid(hg) * _dot(h, wu_ref[...])).astype(BF16)
    acc_ref[...] += _dot(act, wd_ref[...])

    @pl.when(j == pl.num_programs(1) - 1)
    def _():
        out_ref[...] = _ple_tail(x_ref[...] + acc_ref[...], p_ref, gp_ref, wpg_ref, wpp_ref, gf_ref, final)


def _ffn(x, g, w_gu, w_down, ple, layer, idx, final, tm, tf):
    n, d = x.shape
    n_f = w_down.shape[1] // tf
    return pl.pallas_call(
        functools.partial(_ffn_kernel, final=final),
        grid=(n // tm, n_f),
        in_specs=[pl.BlockSpec((tm, d), lambda i, j: (i, 0)),
                  _layer_row(g, layer),
                  pl.BlockSpec((None, d, tf), lambda i, j: (idx, 0, j)),
                  pl.BlockSpec((None, d, tf), lambda i, j: (idx, 0, j + n_f)),
                  pl.BlockSpec((None, tf, d), lambda i, j: (idx, j, 0))] + _ple_specs(ple, layer, tm),
        out_specs=pl.BlockSpec((tm, d), lambda i, j: (i, 0)),
        out_shape=jax.ShapeDtypeStruct((n, d), F32),
        scratch_shapes=[pltpu.VMEM((tm, d), BF16), pltpu.VMEM((tm, d), F32)],
        compiler_params=_params("parallel", "arbitrary"),
        name="ffn",
    )(x, g, w_gu, w_gu, w_down, *ple)


SEG_ALIGN = 8
R_LOCAL, R_WEIGHT = 2, 4
SEG_UNITS, SEG_LOCAL, SEG_BASE = 0, 8, 16


def _local_rows(tm, n_experts):
    return -(-(2 * tm + SEG_ALIGN * n_experts) // LANES) * LANES


def _router_kernel(x_ref, g_ref, wr_ref, route_ref, seg_ref, count_ref, base_ref, *, n_experts):
    @pl.when(pl.program_id(0) == 0)
    def _():
        base_ref[...] = jnp.zeros_like(base_ref)

    tm = x_ref.shape[0]
    h = _rms(x_ref[...], g_ref[...]).astype(BF16)
    lane_i = lax.broadcasted_iota(jnp.int32, (tm, LANES), 1)
    lane = lane_i.astype(F32)
    logits = jnp.where(lane_i < n_experts, _dot(h, wr_ref[...]), NEG_INF)

    def take_max(vals):
        top = jnp.max(vals, axis=-1, keepdims=True)
        idx = jnp.min(jnp.where(vals == top, lane, float(LANES)), axis=-1, keepdims=True)
        return top, idx

    v1, i1 = take_max(logits)
    v2, i2 = take_max(jnp.where(lane == i1, NEG_INF, logits))
    e2 = jnp.exp(v2 - v1)
    w1 = 1.0 / (1.0 + e2)
    hit1, hit2 = lane == i1, lane == i2
    onehot = jnp.where(hit1 | hit2, 1.0, 0.0)
    earlier = (lax.broadcasted_iota(jnp.int32, (tm, tm), 0) > lax.broadcasted_iota(jnp.int32, (tm, tm), 1))
    before = _dot(earlier.astype(BF16), onehot.astype(BF16))
    units = jnp.ceil(jnp.sum(onehot, axis=0, keepdims=True) * (1.0 / SEG_ALIGN))
    lower = (lax.broadcasted_iota(jnp.int32, (LANES, LANES), 0) < lax.broadcasted_iota(jnp.int32, (LANES, LANES), 1))
    local = SEG_ALIGN * _dot(jnp.broadcast_to(units, (SEG_ALIGN, LANES)).astype(BF16), lower.astype(BF16))[:1]
    row = before + local
    l1 = jnp.sum(jnp.where(hit1, row, 0.0), axis=-1, keepdims=True)
    l2 = jnp.sum(jnp.where(hit2, row, 0.0), axis=-1, keepdims=True)
    route = jnp.zeros((tm, LANES), F32)
    for k, val in enumerate((i1, i2, l1, l2, w1, e2 * w1)):
        route = jnp.where(lane_i == k, val, route)
    route_ref[...] = route
    roll_to = lambda a, k: pltpu.roll(a, k, 1)
    first = lane_i[:1] < n_experts
    seg = (jnp.where(first, units, 0.0) + roll_to(jnp.where(first, local, 0.0), SEG_LOCAL)
           + roll_to(jnp.where(first, base_ref[...], 0.0), SEG_BASE))
    seg_ref[...] = seg
    base_ref[...] += SEG_ALIGN * units
    count_ref[...] = base_ref[...]


def _router(x, g, w_router, layer, tm):
    n, d = x.shape
    n_experts = w_router.shape[1]
    assert n_experts <= SEG_LOCAL
    wr = jnp.pad(w_router, ((0, 0), (0, LANES - n_experts))).astype(BF16)
    return pl.pallas_call(
        functools.partial(_router_kernel, n_experts=n_experts),
        grid=(n // tm,),
        in_specs=[pl.BlockSpec((tm, d), lambda i: (i, 0)), _layer_row(g, layer),
                  pl.BlockSpec((d, LANES), lambda i: (0, 0))],
        out_specs=[pl.BlockSpec((tm, LANES), lambda i: (i, 0)),
                   pl.BlockSpec((None, 1, LANES), lambda i: (i, 0, 0)),
                   pl.BlockSpec((1, LANES), lambda i: (0, 0))],
        out_shape=[jax.ShapeDtypeStruct((n, LANES), F32), jax.ShapeDtypeStruct((n // tm, 1, LANES), F32),
                   jax.ShapeDtypeStruct((1, LANES), F32)],
        scratch_shapes=[pltpu.VMEM((1, LANES), F32)],
        compiler_params=_params("arbitrary"),
        name="router",
    )(x, g, wr)


def _for_each_chunk(seg_ref, n_experts, max_units, fn):
    for e in range(n_experts):
        units = seg_ref[0, 0, SEG_UNITS + e]
        local = seg_ref[0, 0, SEG_LOCAL + e]
        grouped = seg_ref[0, 0, SEG_BASE + e]
        for b in range(max_units.bit_length()):
            skip = (units >> (b + 1) << (b + 1)) * SEG_ALIGN

            @pl.when((units >> b) & 1 == 1)
            def _():
                fn(pl.multiple_of(local + skip, SEG_ALIGN), pl.multiple_of(grouped + skip, SEG_ALIGN), SEG_ALIGN << b)


def _dispatch_kernel(pad_ref, seg_ref, prev_seg_ref, x_ref, g_ref, route_ref, xs_ref, sorted_ref, zero_ref, sem,
                     *, n_experts, n_tokens):
    step = pl.program_id(0)
    tm = x_ref.shape[0]
    tmg = zero_ref.shape[0]
    max_tail = (xs_ref.shape[0] - 2 * n_tokens) // tmg
    fill = lambda row: pltpu.make_async_copy(zero_ref, xs_ref.at[pl.ds(pl.multiple_of(row, SEG_ALIGN), tmg)], sem.at[0])

    @pl.when(step == 0)
    def _():
        zero_ref[...] = jnp.zeros_like(zero_ref)
        for e in range(n_experts):
            fill(pad_ref[e]).start()
        for e in range(n_experts):
            fill(pad_ref[e]).wait()
        for t in range(max_tail):
            row = pad_ref[n_experts] + t * tmg

            @pl.when(row < xs_ref.shape[0])
            def _():
                fill(row).start()
                fill(row).wait()

    slot = step % 2
    n_local = sorted_ref.shape[1]
    h = _rms(x_ref[...], g_ref[...]).astype(BF16)
    owner = route_ref[...].T
    row = lax.broadcasted_iota(jnp.int32, (n_local, tm), 0).astype(F32)
    pick = (row == owner[R_LOCAL:R_LOCAL + 1]) | (row == owner[R_LOCAL + 1:R_LOCAL + 2])
    sorted_ref[slot] = _dot(jnp.where(pick, 1.0, 0.0).astype(BF16), h)

    def segment_copy(buf_slot):
        return lambda local, grouped, rows: pltpu.make_async_copy(
            sorted_ref.at[buf_slot, pl.ds(local, rows)], xs_ref.at[pl.ds(grouped, rows)], sem.at[buf_slot])

    _for_each_chunk(seg_ref, n_experts, tm // SEG_ALIGN, lambda *a: segment_copy(slot)(*a).start())

    @pl.when(step > 0)
    def _():
        _for_each_chunk(prev_seg_ref, n_experts, tm // SEG_ALIGN, lambda *a: segment_copy(1 - slot)(*a).wait())

    @pl.when(step == pl.num_programs(0) - 1)
    def _():
        _for_each_chunk(seg_ref, n_experts, tm // SEG_ALIGN, lambda *a: segment_copy(slot)(*a).wait())


def _dispatch(x, g, layer, route, seg, pad_start, rows, tm, tmg):
    n, d = x.shape
    n_experts = pad_start.shape[0] - 1
    seg_spec = lambda shift: pl.BlockSpec((1, 1, seg.shape[2]), lambda i, pad: (jnp.maximum(i - shift, 0), 0, 0),
                                          memory_space=pltpu.SMEM)
    return pl.pallas_call(
        functools.partial(_dispatch_kernel, n_experts=n_experts, n_tokens=n),
        grid_spec=pltpu.PrefetchScalarGridSpec(
            num_scalar_prefetch=1,
            grid=(n // tm,),
            in_specs=[seg_spec(0), seg_spec(1),
                      pl.BlockSpec((tm, d), lambda i, pad: (i, 0)),
                      _layer_row(g, layer),
                      pl.BlockSpec((tm, LANES), lambda i, pad: (i, 0))],
            out_specs=pl.BlockSpec(memory_space=pl.ANY),
            scratch_shapes=[pltpu.VMEM((2, _local_rows(tm, n_experts), d), F32), pltpu.VMEM((tmg, d), F32),
                            pltpu.SemaphoreType.DMA((2,))],
        ),
        out_shape=jax.ShapeDtypeStruct((rows + tmg, d), F32),
        compiler_params=_params("arbitrary"),
        name="dispatch",
    )(pad_start, seg, seg, x, g, route)


def _expert_kernel(te_ref, used_ref, xs_ref, wg_ref, wu_ref, wd_ref, y_ref):
    @pl.when(pl.program_id(0) < used_ref[0])
    def _():
        h = xs_ref[...].astype(BF16)
        hg = _dot(h, wg_ref[0])
        act = (hg * _sigmoid(hg) * _dot(h, wu_ref[0])).astype(BF16)
        y_ref[...] = _dot(act, wd_ref[0])

    @pl.when(pl.program_id(0) >= used_ref[0])
    def _():
        y_ref[...] = jnp.zeros_like(y_ref)


def _experts(xs, tile_expert, n_used, w_gu, w_down, idx, rows, tmg):
    d = xs.shape[1]
    d_ff = w_down.shape[2]
    live = lambda t, te, used: jnp.minimum(t, used[0] - 1)
    return pl.pallas_call(
        _expert_kernel,
        grid_spec=pltpu.PrefetchScalarGridSpec(
            num_scalar_prefetch=2,
            grid=(rows // tmg,),
            in_specs=[pl.BlockSpec((tmg, d), lambda t, te, used: (live(t, te, used), 0)),
                      pl.BlockSpec((None, 1, d, d_ff), lambda t, te, used: (idx, te[live(t, te, used)], 0, 0)),
                      pl.BlockSpec((None, 1, d, d_ff), lambda t, te, used: (idx, te[live(t, te, used)], 0, 1)),
                      pl.BlockSpec((None, 1, d_ff, d), lambda t, te, used: (idx, te[live(t, te, used)], 0, 0))],
            out_specs=pl.BlockSpec((tmg, d), lambda t, te, used: (t, 0)),
        ),
        out_shape=jax.ShapeDtypeStruct((rows, d), F32),
        compiler_params=_params("arbitrary"),
        name="experts",
    )(tile_expert, n_used, xs, w_gu, w_gu, w_down)


def _combine_kernel(seg_ref, next_seg_ref, x_ref, route_ref, y_ref, p_ref, gp_ref, wpg_ref, wpp_ref, gf_ref,
                    out_ref, buf_ref, sem, *, n_experts, final):
    step = pl.program_id(0)
    tm = x_ref.shape[0]
    slot = step % 2
    n_local = buf_ref.shape[1]

    def segment_copy(buf_slot):
        return lambda local, grouped, rows: pltpu.make_async_copy(
            y_ref.at[pl.ds(grouped, rows)], buf_ref.at[buf_slot, pl.ds(local, rows)], sem.at[buf_slot])

    @pl.when(step == 0)
    def _():
        buf_ref[...] = jnp.zeros_like(buf_ref)
        _for_each_chunk(seg_ref, n_experts, tm // SEG_ALIGN, lambda *a: segment_copy(slot)(*a).start())

    @pl.when(step < pl.num_programs(0) - 1)
    def _():
        _for_each_chunk(next_seg_ref, n_experts, tm // SEG_ALIGN, lambda *a: segment_copy(1 - slot)(*a).start())

    _for_each_chunk(seg_ref, n_experts, tm // SEG_ALIGN, lambda *a: segment_copy(slot)(*a).wait())

    route = route_ref[...]
    y = buf_ref[slot].astype(BF16)
    col = lax.broadcasted_iota(jnp.int32, (tm, n_local), 1).astype(F32)
    weigh = jnp.zeros((tm, n_local), F32)
    for k in range(2):
        weigh = jnp.where(col == route[:, R_LOCAL + k:R_LOCAL + k + 1], route[:, R_WEIGHT + k:R_WEIGHT + k + 1], weigh)
    x2 = x_ref[...] + _dot(weigh.astype(BF16), y)
    out_ref[...] = _ple_tail(x2, p_ref, gp_ref, wpg_ref, wpp_ref, gf_ref, final)


def _combine(x, route, seg, y, ple, layer, final, tm):
    n, d = x.shape
    n_experts = SEG_LOCAL - SEG_UNITS
    last = n // tm - 1
    seg_spec = lambda shift: pl.BlockSpec((1, 1, seg.shape[2]), lambda i: (jnp.minimum(i + shift, last), 0, 0),
                                          memory_space=pltpu.SMEM)
    return pl.pallas_call(
        functools.partial(_combine_kernel, n_experts=n_experts, final=final),
        grid=(n // tm,),
        in_specs=[seg_spec(0), seg_spec(1),
                  pl.BlockSpec((tm, d), lambda i: (i, 0)),
                  pl.BlockSpec((tm, LANES), lambda i: (i, 0)),
                  pl.BlockSpec(memory_space=pl.ANY)] + _ple_specs(ple, layer, tm),
        out_specs=pl.BlockSpec((tm, d), lambda i: (i, 0)),
        out_shape=jax.ShapeDtypeStruct((n, d), F32),
        scratch_shapes=[pltpu.VMEM((2, _local_rows(tm, n_experts), d), F32), pltpu.SemaphoreType.DMA((2,))],
        compiler_params=_params("arbitrary"),
        name="combine",
    )(seg, seg, x, route, y, *ple)


def _moe(x, g, layer, w_router, w_gu, w_down, idx, ple, final, tm, tmg):
    n = x.shape[0]
    n_experts = w_router.shape[1]
    route, seg, count = _router(x, g, w_router, layer, tm)
    counts = count[0, :n_experts].astype(jnp.int32)
    padded = (counts + tmg - 1) // tmg * tmg
    ends = jnp.cumsum(padded)
    starts = ends - padded
    seg = seg.astype(jnp.int32)[:, :, :4 * SEG_LOCAL]
    seg = seg.at[:, 0, SEG_BASE:SEG_BASE + n_experts].add(starts)
    rows = -(-(2 * n + (n // tm) * n_experts * SEG_ALIGN) // tmg) * tmg + n_experts * tmg
    n_used = (ends[-1:] // tmg).astype(jnp.int32)
    tile_row = jnp.arange(rows // tmg, dtype=jnp.int32) * tmg
    tile_expert = jnp.minimum(jnp.sum(tile_row[:, None] >= ends[None, :], axis=-1), n_experts - 1).astype(jnp.int32)
    pad_start = jnp.concatenate([starts + counts, ends[-1:]]).astype(jnp.int32)

    xs = _dispatch(x, g, layer, route, seg, pad_start, rows, tm, tmg)
    y = _experts(xs, tile_expert, n_used, w_gu, w_down, idx, rows, tmg)
    return _combine(x, route, seg, y, ple, layer, final, tm)


def _tiles(n, seq):
    return min(512, n), min(1024, n), min(512, seq), min(512, n), 512


def _ff_tile(d_ff):
    for tf in (512, 896, 256, 128):
        if d_ff % tf == 0:
            return tf
    raise ValueError(f"unsupported hidden width {d_ff}")


def kernel(x, p, positions, norm_mix_g, w_in, conv_w, pool_w, pool_scale, attn_sinks, w_branch, w_out, norm_ffn_g, ffn_w_gu, ffn_w_down, moe_router, moe_w_gu, moe_w_down, norm_ple_g, ple_w_proj, ple_w_gate, final_norm_g):
    batch, seq, d = x.shape
    depth = w_in.shape[0]
    n = batch * seq
    tm_in, tm, tile, tm_row, tmg = _tiles(n, seq)
    assert seq % tile == 0 and tile % WINDOW == 0 and n % tm == 0 and n % tm_in == 0 and n % tm_row == 0
    stack_rows = lambda a: a.reshape(a.shape[0], 1, a.shape[1])

    w_in, pool_w, w_branch, w_out = (a.astype(BF16) for a in (w_in, pool_w, w_branch, w_out))
    ffn_w_gu, ffn_w_down, moe_w_gu, moe_w_down = (a.astype(BF16) for a in (ffn_w_gu, ffn_w_down, moe_w_gu, moe_w_down))
    ple_w_gate, ple_w_proj = ple_w_gate.astype(BF16), ple_w_proj.astype(BF16)
    g_mix, g_ffn, g_ple = stack_rows(norm_mix_g), stack_rows(norm_ffn_g), stack_rows(norm_ple_g)
    pool_scale = stack_rows(pool_scale)
    ple = (p.reshape(depth, n, -1), g_ple, ple_w_gate, ple_w_proj, final_norm_g.reshape(1, d))

    cos, sin = _rope_tables(positions, tm)
    xs = x.reshape(n, d)
    for i in range(depth):
        conv, pq, kv, gates = _in_proj(xs, g_mix, cos, sin, w_in, i, tm_in)
        xs = _mixer(xs, conv, pq, kv, gates, attn_sinks, conv_w, pool_w, pool_scale, w_branch, w_out, i, batch, tile)
        final = i == depth - 1
        if i % 2 == 0:
            xs = _ffn(xs, g_ffn, ffn_w_gu, ffn_w_down, ple, i, i // 2, final, tm, _ff_tile(ffn_w_down.shape[1]))
        else:
            xs = _moe(xs, g_ffn, i, moe_router[i // 2], moe_w_gu, moe_w_down, i // 2, ple, final, tm_row, tmg)
    return xs.reshape(batch, seq, d)
```

```python
import functools

import jax
import jax.numpy as jnp
from jax import lax
from jax.experimental import pallas as pl
from jax.experimental.pallas import tpu as pltpu

F32 = jnp.float32
BF16 = jnp.bfloat16

BRANCH_W = 512
N_BRANCH = 3
POOL_WINDOWS = (2, 4, 8, 16)
POOL_GROUP_W = 128
N_HEADS = 8
N_KV_HEADS = 2
HEADS_PER_KV = N_HEADS // N_KV_HEADS
HEAD_DIM = 64
HALF = HEAD_DIM // 2
KV_W = N_KV_HEADS * HEAD_DIM
WINDOW = 128
ROPE_THETA = 10000.0
NEG_INF = -1e30
EPS = 1e-6
OFF_POOL = 3 * BRANCH_W
OFF_Q = OFF_POOL + BRANCH_W
OFF_K = OFF_Q + N_HEADS * HEAD_DIM
OFF_GATE = OFF_K + 2 * KV_W

LANES = 128
V7X_MXU_WIDTH = 256
FFN_CHUNK = 2 * V7X_MXU_WIDTH
V7X_VMEM_LIMIT_BYTES = 56 * 1024 * 1024
HALO = 16


def _params(*sem):
    return pltpu.CompilerParams(dimension_semantics=sem, vmem_limit_bytes=V7X_VMEM_LIMIT_BYTES)


def _rms(x, g):
    ms = jnp.mean(x * x, axis=-1, keepdims=True)
    return x * lax.rsqrt(ms + EPS) * g


def _sigmoid(x):
    return 0.5 * jnp.tanh(0.5 * x) + 0.5


def _dot(a, b):
    return jnp.dot(a, b, preferred_element_type=F32)


def _ple_tail(x, p_ref, g_ref, wg_ref, wp_ref, gf_ref, final):
    h = _rms(x, g_ref[...]).astype(BF16)
    y = x + _sigmoid(_dot(h, wg_ref[...])) * _dot(p_ref[...].astype(BF16), wp_ref[...])
    return _rms(y, gf_ref[...]) if final else y


def _ple_specs(ple, layer, tm):
    p, g, w_gate, w_proj, g_final = ple
    return [pl.BlockSpec((None, tm, p.shape[2]), lambda i, *_: (layer, i, 0)), _layer_row(g, layer),
            _layer_full(w_gate, layer), _layer_full(w_proj, layer),
            pl.BlockSpec(g_final.shape, lambda *_: (0, 0))]


def _layer_row(stacked, layer):
    return pl.BlockSpec((None, 1, stacked.shape[2]), lambda *_: (layer, 0, 0))


def _layer_full(stacked, layer):
    shape = stacked.shape[1:]
    return pl.BlockSpec((None,) + shape, lambda *_: (layer,) + (0,) * len(shape))


def _rope_kernel(pos_ref, freq_ref, sign_ref, cos_ref, sin_ref):
    ang = pos_ref[...].astype(F32) * freq_ref[...]
    cos_ref[...] = jnp.cos(ang)
    sin_ref[...] = jnp.sin(ang) * sign_ref[...]


def _rope_tables(positions, tm):
    n = positions.size
    inv_freq = ROPE_THETA ** (-2.0 * jnp.arange(HALF, dtype=F32) / HEAD_DIM)
    freq = jnp.tile(inv_freq, LANES // HALF).reshape(1, LANES)
    sign = jnp.tile(jnp.concatenate([-jnp.ones(HALF, F32), jnp.ones(HALF, F32)]), LANES // HEAD_DIM).reshape(1, LANES)
    row = pl.BlockSpec((1, LANES), lambda i: (0, 0))
    tab = pl.BlockSpec((tm, LANES), lambda i: (i, 0))
    return pl.pallas_call(
        _rope_kernel,
        grid=(n // tm,),
        in_specs=[pl.BlockSpec((tm, 1), lambda i: (i, 0)), row, row],
        out_specs=[tab, tab],
        out_shape=[jax.ShapeDtypeStruct((n, LANES), F32)] * 2,
        compiler_params=_params("parallel"),
        name="rope_tables",
    )(positions.reshape(n, 1), freq, sign)


def _rotate_half(x, cos, sin_signed):
    width = x.shape[-1]
    lane = lax.broadcasted_iota(jnp.int32, x.shape, 1)
    first = (lane % HEAD_DIM) < HALF
    partner = jnp.where(first, pltpu.roll(x, width - HALF, 1), pltpu.roll(x, HALF, 1))
    return x * cos + partner * sin_signed


def _in_proj_kernel(x_ref, g_ref, cos_ref, sin_ref, w_ref, conv_ref, pq_ref, kv_ref, gate_ref):
    h = _rms(x_ref[...], g_ref[...]).astype(BF16)

    def proj(c0, width):
        return _dot(h, w_ref[:, c0:c0 + width])

    conv_ref[:, :BRANCH_W] = proj(0, BRANCH_W).astype(BF16)
    conv_ref[:, BRANCH_W:] = (proj(BRANCH_W, BRANCH_W) * proj(2 * BRANCH_W, BRANCH_W)).astype(BF16)
    pq_ref[:, :BRANCH_W] = proj(OFF_POOL, BRANCH_W).astype(BF16)
    cos = cos_ref[...]
    sin = sin_ref[...]
    rep = N_HEADS * HEAD_DIM // LANES
    q = _rotate_half(proj(OFF_Q, N_HEADS * HEAD_DIM), jnp.concatenate([cos] * rep, axis=1),
                     jnp.concatenate([sin] * rep, axis=1))
    pq_ref[:, BRANCH_W:] = (q * HEAD_DIM ** -0.5).astype(BF16)
    kv = proj(OFF_K, 2 * KV_W)
    kv_ref[:, :KV_W] = _rotate_half(kv[:, :KV_W], cos, sin).astype(BF16)
    kv_ref[:, KV_W:] = kv[:, KV_W:].astype(BF16)
    d = gate_ref.shape[1] // N_BRANCH
    for r in range(N_BRANCH):
        gate_ref[:, r * d:(r + 1) * d] = _sigmoid(proj(OFF_GATE + r * d, d)).astype(BF16)


def _in_proj(x, g, cos, sin, w_in, layer, tm):
    n, d = x.shape
    d_in = w_in.shape[2]
    tok = lambda width: pl.BlockSpec((tm, width), lambda i: (i, 0))
    widths = (2 * BRANCH_W, OFF_Q + N_HEADS * HEAD_DIM - OFF_POOL, 2 * KV_W, N_BRANCH * d)
    return pl.pallas_call(
        _in_proj_kernel,
        grid=(n // tm,),
        in_specs=[tok(d), _layer_row(g, layer), tok(LANES), tok(LANES),
                  pl.BlockSpec((None, d, d_in), lambda i: (layer, 0, 0), pipeline_mode=pl.Buffered(1))],
        out_specs=[tok(w) for w in widths],
        out_shape=[jax.ShapeDtypeStruct((n, w), BF16) for w in widths],
        compiler_params=_params("parallel"),
        name="in_proj",
    )(x, g, cos, sin, w_in)


def _mixer_kernel(sinks_ref, conv_ref, convh_ref, pq_ref, poolh_ref, kv_ref, kvh_ref, gate_ref, x_ref,
                  convw_ref, poolw_ref, pscale_ref, wbr_ref, wout_ref, out_ref, att_ref, *, tile, layer):
    seq_tile = pl.program_id(1)
    not_first = seq_tile > 0
    w = BRANCH_W

    v = conv_ref[:, w:].astype(F32)
    vh = convh_ref[:, w:].astype(F32)
    ext = jnp.concatenate([jnp.where(not_first, vh, 0.0), v], axis=0)
    cw = convw_ref[...]
    y = cw[2:3] * v + cw[1:2] * pltpu.roll(ext, 1, 0)[HALO:] + cw[0:1] * pltpu.roll(ext, 2, 0)[HALO:]
    y_conv = (conv_ref[:, :w].astype(F32) * y).astype(BF16)

    pu = pq_ref[:, :w].astype(F32)
    s = jnp.concatenate([jnp.where(not_first, poolh_ref[...].astype(F32), 0.0), pu], axis=0)
    pos1 = seq_tile * tile + lax.broadcasted_iota(jnp.int32, (tile, 1), 0) + 1
    mixed = []
    for gi, win in enumerate(POOL_WINDOWS):
        s = s[:, (POOL_GROUP_W if gi else 0):]
        s = s + pltpu.roll(s, win // 2, 0)
        inv_count = 1.0 / jnp.minimum(pos1, win).astype(F32)
        lo_c = gi * POOL_GROUP_W
        pooled = s[HALO:, :POOL_GROUP_W] * inv_count - pu[:, lo_c:lo_c + POOL_GROUP_W]
        mixed.append(_dot(pooled.astype(BF16), poolw_ref[gi]))
    y_pool = (jnp.concatenate(mixed, axis=1) * pscale_ref[...]).astype(BF16)

    d = x_ref.shape[1]
    gated = lambda r, br: gate_ref[:, r * d:(r + 1) * d].astype(F32) * _dot(br, wbr_ref[r])
    early = {0: y_conv, 1: y_pool}
    merged = []

    kvh = jnp.where(not_first, kvh_ref[...], jnp.zeros_like(kvh_ref[...]))
    kext = jnp.concatenate([kvh, kv_ref[...]], axis=0)
    qw = HEADS_PER_KV * HEAD_DIM
    rows = HEADS_PER_KV * WINDOW
    lo = lax.broadcasted_iota(jnp.int32, (2 * WINDOW, KV_W), 1) < HEAD_DIM
    lane_head = lax.broadcasted_iota(jnp.int32, (WINDOW, qw), 1) // HEAD_DIM
    qi = lax.broadcasted_iota(jnp.int32, (rows, 2 * WINDOW), 0) % WINDOW
    key = lax.broadcasted_iota(jnp.int32, (rows, 2 * WINDOW), 1)
    dist = qi + WINDOW - key
    in_band = (dist >= 0) & (dist < WINDOW)
    row_head = lax.broadcasted_iota(jnp.int32, (rows, 1), 0) // WINDOW
    for j in range(tile // WINDOW):
        band = kext[j * WINDOW:(j + 2) * WINDOW].astype(F32)
        halves = [(t, pltpu.roll(t, HEAD_DIM, 1)) for t in (band[:, :KV_W], band[:, KV_W:])]
        valid = in_band if j else in_band & ((key >= WINDOW) | not_first)
        for g in range(N_KV_HEADS):
            k_rep, v_rep = (
                jnp.concatenate([jnp.where(lo, r, t) if g else jnp.where(lo, t, r)] * (qw // KV_W), axis=1).astype(BF16)
                for t, r in halves)
            qg = pq_ref[j * WINDOW:(j + 1) * WINDOW, w + g * qw:w + (g + 1) * qw]
            q_heads = jnp.concatenate(
                [jnp.where(lane_head == hh, qg, jnp.zeros_like(qg)) for hh in range(HEADS_PER_KV)], axis=0)
            scores = lax.dot_general(q_heads, k_rep, (((1,), (1,)), ((), ())), preferred_element_type=F32)
            sink = jnp.zeros((rows, 1), F32)
            for hh in range(HEADS_PER_KV):
                sink = jnp.where(row_head == hh, sinks_ref[layer, g * HEADS_PER_KV + hh], sink)
            sh = jnp.where(valid, scores, NEG_INF)
            m = jnp.maximum(jnp.max(sh, axis=-1, keepdims=True), sink)
            pexp = jnp.exp(sh - m)
            denom = jnp.sum(pexp, axis=-1, keepdims=True) + jnp.exp(sink - m)
            res = _dot((pexp * (1.0 / denom)).astype(BF16), v_rep)
            out = res[:WINDOW]
            for hh in range(1, HEADS_PER_KV):
                out = jnp.where(lane_head == hh, res[hh * WINDOW:(hh + 1) * WINDOW], out)
            att_ref[j * WINDOW:(j + 1) * WINDOW, g * qw:(g + 1) * qw] = out.astype(BF16)
        if j in early:
            merged.append(gated(j, early[j]))

    merged = merged[0] + merged[1] + gated(2, att_ref[...])
    out_ref[...] = x_ref[...] + _dot(merged.astype(BF16), wout_ref[...])


def _mixer(x, conv, pq, kv, gates, sinks, conv_w, pool_w, pool_scale, w_branch, w_out, layer, batch, tile):
    n, d = x.shape
    n_t = n // batch // tile
    t_idx = lambda b, i: b * n_t + i
    tok = lambda width: pl.BlockSpec((tile, width), lambda b, i: (t_idx(b, i), 0))
    prev = lambda rows, width: pl.BlockSpec(
        (rows, width), lambda b, i: (jnp.maximum(t_idx(b, i) * (tile // rows) - 1, 0), 0))
    return pl.pallas_call(
        functools.partial(_mixer_kernel, tile=tile, layer=layer),
        grid=(batch, n_t),
        in_specs=[pl.BlockSpec(memory_space=pltpu.SMEM),
                  tok(conv.shape[1]), prev(HALO, conv.shape[1]),
                  tok(pq.shape[1]), prev(HALO, BRANCH_W),
                  tok(kv.shape[1]), prev(WINDOW, kv.shape[1]),
                  tok(gates.shape[1]), tok(d),
                  _layer_full(conv_w, layer), _layer_full(pool_w, layer), _layer_row(pool_scale, layer),
                  _layer_full(w_branch, layer), _layer_full(w_out, layer)],
        out_specs=tok(d),
        out_shape=jax.ShapeDtypeStruct((n, d), F32),
        scratch_shapes=[pltpu.VMEM((tile, N_HEADS * HEAD_DIM), BF16)],
        compiler_params=_params("parallel", "parallel"),
        name="mixer",
    )(sinks, conv, conv, pq, pq, kv, kv, gates, x, conv_w, pool_w, pool_scale, w_branch, w_out)


def _ffn_kernel(x_ref, g_ref, wg_ref, wu_ref, wd_ref, p_ref, gp_ref, wpg_ref, wpp_ref, gf_ref, out_ref, h_ref, acc_ref,
                *, final):
    j = pl.program_id(1)

    @pl.when(j == 0)
    def _():
        h_ref[...] = _rms(x_ref[...], g_ref[...]).astype(BF16)
        acc_ref[...] = jnp.zeros_like(acc_ref)

    h = h_ref[...]
    acts = []
    tf = wg_ref.shape[1]
    for c in range(0, tf, FFN_CHUNK):
        cols = slice(c, min(c + FFN_CHUNK, tf))
        hg = _dot(h, wg_ref[:, cols])
        acts.append((hg * _sigmoid(hg) * _dot(h, wu_ref[:, cols])).astype(BF16))
    acc_ref[...] += _dot(jnp.concatenate(acts, axis=1), wd_ref[...])

    @pl.when(j == pl.num_programs(1) - 1)
    def _():
        out_ref[...] = _ple_tail(x_ref[...] + acc_ref[...], p_ref, gp_ref, wpg_ref, wpp_ref, gf_ref, final)


def _ffn(x, g, w_gu, w_down, ple, layer, idx, final, tm, tf):
    n, d = x.shape
    n_f = w_down.shape[1] // tf
    return pl.pallas_call(
        functools.partial(_ffn_kernel, final=final),
        grid=(n // tm, n_f),
        in_specs=[pl.BlockSpec((tm, d), lambda i, j: (i, 0)),
                  _layer_row(g, layer),
                  pl.BlockSpec((None, d, tf), lambda i, j: (idx, 0, j)),
                  pl.BlockSpec((None, d, tf), lambda i, j: (idx, 0, j + n_f)),
                  pl.BlockSpec((None, tf, d), lambda i, j: (idx, j, 0))] + _ple_specs(ple, layer, tm),
        out_specs=pl.BlockSpec((tm, d), lambda i, j: (i, 0)),
        out_shape=jax.ShapeDtypeStruct((n, d), F32),
        scratch_shapes=[pltpu.VMEM((tm, d), BF16), pltpu.VMEM((tm, d), F32)],
        compiler_params=_params("parallel", "arbitrary"),
        name="ffn",
    )(x, g, w_gu, w_gu, w_down, *ple)


SEG_ALIGN = 8
R_LOCAL, R_WEIGHT = 2, 4
SEG_UNITS, SEG_LOCAL, SEG_BASE = 0, 8, 16


def _local_rows(tm, n_experts):
    return -(-(2 * tm + SEG_ALIGN * n_experts) // LANES) * LANES


def _router_kernel(x_ref, g_ref, wr_ref, route_ref, seg_ref, count_ref, base_ref, *, n_experts):
    @pl.when(pl.program_id(0) == 0)
    def _():
        base_ref[...] = jnp.zeros_like(base_ref)

    tm = x_ref.shape[0]
    h = _rms(x_ref[...], g_ref[...]).astype(BF16)
    lane_i = lax.broadcasted_iota(jnp.int32, (tm, LANES), 1)
    lane = lane_i.astype(F32)
    logits = jnp.where(lane_i < n_experts, _dot(h, wr_ref[...]), NEG_INF)

    def take_max(vals):
        top = jnp.max(vals, axis=-1, keepdims=True)
        idx = jnp.min(jnp.where(vals == top, lane, float(LANES)), axis=-1, keepdims=True)
        return top, idx

    v1, i1 = take_max(logits)
    v2, i2 = take_max(jnp.where(lane == i1, NEG_INF, logits))
    e2 = jnp.exp(v2 - v1)
    w1 = 1.0 / (1.0 + e2)
    hit1, hit2 = lane == i1, lane == i2
    onehot = jnp.where(hit1 | hit2, 1.0, 0.0)
    earlier = (lax.broadcasted_iota(jnp.int32, (tm, tm), 0) > lax.broadcasted_iota(jnp.int32, (tm, tm), 1))
    before = _dot(earlier.astype(BF16), onehot.astype(BF16))
    units = jnp.ceil(jnp.sum(onehot, axis=0, keepdims=True) * (1.0 / SEG_ALIGN))
    lower = (lax.broadcasted_iota(jnp.int32, (LANES, LANES), 0) < lax.broadcasted_iota(jnp.int32, (LANES, LANES), 1))
    local = SEG_ALIGN * _dot(jnp.broadcast_to(units, (SEG_ALIGN, LANES)).astype(BF16), lower.astype(BF16))[:1]
    row = before + local
    l1 = jnp.sum(jnp.where(hit1, row, 0.0), axis=-1, keepdims=True)
    l2 = jnp.sum(jnp.where(hit2, row, 0.0), axis=-1, keepdims=True)
    route = jnp.zeros((tm, LANES), F32)
    for k, val in enumerate((i1, i2, l1, l2, w1, e2 * w1)):
        route = jnp.where(lane_i == k, val, route)
    route_ref[...] = route
    roll_to = lambda a, k: pltpu.roll(a, k, 1)
    first = lane_i[:1] < n_experts
    seg = (jnp.where(first, units, 0.0) + roll_to(jnp.where(first, local, 0.0), SEG_LOCAL)
           + roll_to(jnp.where(first, base_ref[...], 0.0), SEG_BASE))
    seg_ref[...] = seg
    base_ref[...] += SEG_ALIGN * units
    count_ref[...] = base_ref[...]


def _router(x, g, w_router, layer, tm):
    n, d = x.shape
    n_experts = w_router.shape[1]
    assert n_experts <= SEG_LOCAL
    wr = jnp.pad(w_router, ((0, 0), (0, LANES - n_experts))).astype(BF16)
    return pl.pallas_call(
        functools.partial(_router_kernel, n_experts=n_experts),
        grid=(n // tm,),
        in_specs=[pl.BlockSpec((tm, d), lambda i: (i, 0)), _layer_row(g, layer),
                  pl.BlockSpec((d, LANES), lambda i: (0, 0))],
        out_specs=[pl.BlockSpec((tm, LANES), lambda i: (i, 0)),
                   pl.BlockSpec((None, 1, LANES), lambda i: (i, 0, 0)),
                   pl.BlockSpec((1, LANES), lambda i: (0, 0))],
        out_shape=[jax.ShapeDtypeStruct((n, LANES), F32), jax.ShapeDtypeStruct((n // tm, 1, LANES), F32),
                   jax.ShapeDtypeStruct((1, LANES), F32)],
        scratch_shapes=[pltpu.VMEM((1, LANES), F32)],
        compiler_params=_params("arbitrary"),
        name="router",
    )(x, g, wr)


def _for_each_chunk(seg_ref, n_experts, max_units, fn):
    for e in range(n_experts):
        units = seg_ref[0, 0, SEG_UNITS + e]
        local = seg_ref[0, 0, SEG_LOCAL + e]
        grouped = seg_ref[0, 0, SEG_BASE + e]
        for b in range(max_units.bit_length()):
            skip = (units >> (b + 1) << (b + 1)) * SEG_ALIGN

            @pl.when((units >> b) & 1 == 1)
            def _():
                fn(pl.multiple_of(local + skip, SEG_ALIGN), pl.multiple_of(grouped + skip, SEG_ALIGN), SEG_ALIGN << b)


def _dispatch_kernel(pad_ref, seg_ref, prev_seg_ref, x_ref, g_ref, route_ref, xs_ref, sorted_ref, zero_ref, sem,
                     *, n_experts, n_tokens):
    step = pl.program_id(0)
    tm = x_ref.shape[0]
    tmg = zero_ref.shape[0]
    max_tail = (xs_ref.shape[0] - 2 * n_tokens) // tmg
    fill = lambda row: pltpu.make_async_copy(zero_ref, xs_ref.at[pl.ds(pl.multiple_of(row, SEG_ALIGN), tmg)], sem.at[0])

    @pl.when(step == 0)
    def _():
        zero_ref[...] = jnp.zeros_like(zero_ref)
        for e in range(n_experts):
            fill(pad_ref[e]).start()
        for e in range(n_experts):
            fill(pad_ref[e]).wait()
        for t in range(max_tail):
            row = pad_ref[n_experts] + t * tmg

            @pl.when(row < xs_ref.shape[0])
            def _():
                fill(row).start()
                fill(row).wait()

    slot = step % 2
    n_local = sorted_ref.shape[1]
    h = _rms(x_ref[...], g_ref[...]).astype(BF16)
    owner = route_ref[...].T
    row = lax.broadcasted_iota(jnp.int32, (n_local, tm), 0).astype(F32)
    pick = (row == owner[R_LOCAL:R_LOCAL + 1]) | (row == owner[R_LOCAL + 1:R_LOCAL + 2])
    sorted_ref[slot] = _dot(jnp.where(pick, 1.0, 0.0).astype(BF16), h)

    def segment_copy(buf_slot):
        return lambda local, grouped, rows: pltpu.make_async_copy(
            sorted_ref.at[buf_slot, pl.ds(local, rows)], xs_ref.at[pl.ds(grouped, rows)], sem.at[buf_slot])

    _for_each_chunk(seg_ref, n_experts, tm // SEG_ALIGN, lambda *a: segment_copy(slot)(*a).start())

    @pl.when(step > 0)
    def _():
        _for_each_chunk(prev_seg_ref, n_experts, tm // SEG_ALIGN, lambda *a: segment_copy(1 - slot)(*a).wait())

    @pl.when(step == pl.num_programs(0) - 1)
    def _():
        _for_each_chunk(seg_ref, n_experts, tm // SEG_ALIGN, lambda *a: segment_copy(slot)(*a).wait())


def _dispatch(x, g, layer, route, seg, pad_start, rows, tm, tmg):
    n, d = x.shape
    n_experts = pad_start.shape[0] - 1
    seg_spec = lambda shift: pl.BlockSpec((1, 1, seg.shape[2]), lambda i, pad: (jnp.maximum(i - shift, 0), 0, 0),
                                          memory_space=pltpu.SMEM)
    return pl.pallas_call(
        functools.partial(_dispatch_kernel, n_experts=n_experts, n_tokens=n),
        grid_spec=pltpu.PrefetchScalarGridSpec(
            num_scalar_prefetch=1,
            grid=(n // tm,),
            in_specs=[seg_spec(0), seg_spec(1),
                      pl.BlockSpec((tm, d), lambda i, pad: (i, 0)),
                      _layer_row(g, layer),
                      pl.BlockSpec((tm, LANES), lambda i, pad: (i, 0))],
            out_specs=pl.BlockSpec(memory_space=pl.ANY),
            scratch_shapes=[pltpu.VMEM((2, _local_rows(tm, n_experts), d), F32), pltpu.VMEM((tmg, d), F32),
                            pltpu.SemaphoreType.DMA((2,))],
        ),
        out_shape=jax.ShapeDtypeStruct((rows + tmg, d), F32),
        compiler_params=_params("arbitrary"),
        name="dispatch",
    )(pad_start, seg, seg, x, g, route)


def _expert_kernel(te_ref, used_ref, xs_ref, wg_ref, wu_ref, wd_ref, y_ref):
    @pl.when(pl.program_id(0) < used_ref[0])
    def _():
        h = xs_ref[...].astype(BF16)
        hg = _dot(h, wg_ref[0])
        act = (hg * _sigmoid(hg) * _dot(h, wu_ref[0])).astype(BF16)
        y_ref[...] = _dot(act, wd_ref[0])

    @pl.when(pl.program_id(0) >= used_ref[0])
    def _():
        y_ref[...] = jnp.zeros_like(y_ref)


def _experts(xs, tile_expert, n_used, w_gu, w_down, idx, rows, tmg):
    d = xs.shape[1]
    d_ff = w_down.shape[2]
    live = lambda t, te, used: jnp.minimum(t, used[0] - 1)
    return pl.pallas_call(
        _expert_kernel,
        grid_spec=pltpu.PrefetchScalarGridSpec(
            num_scalar_prefetch=2,
            grid=(rows // tmg,),
            in_specs=[pl.BlockSpec((tmg, d), lambda t, te, used: (live(t, te, used), 0)),
                      pl.BlockSpec((None, 1, d, d_ff), lambda t, te, used: (idx, te[live(t, te, used)], 0, 0)),
                      pl.BlockSpec((None, 1, d, d_ff), lambda t, te, used: (idx, te[live(t, te, used)], 0, 1)),
                      pl.BlockSpec((None, 1, d_ff, d), lambda t, te, used: (idx, te[live(t, te, used)], 0, 0))],
            out_specs=pl.BlockSpec((tmg, d), lambda t, te, used: (t, 0)),
        ),
        out_shape=jax.ShapeDtypeStruct((rows, d), F32),
        compiler_params=_params("arbitrary"),
        name="experts",
    )(tile_expert, n_used, xs, w_gu, w_gu, w_down)


def _combine_kernel(seg_ref, next_seg_ref, x_ref, route_ref, y_ref, p_ref, gp_ref, wpg_ref, wpp_ref, gf_ref,
                    out_ref, buf_ref, sem, *, n_experts, final):
    step = pl.program_id(0)
    tm = x_ref.shape[0]
    slot = step % 2
    n_local = buf_ref.shape[1]

    def segment_copy(buf_slot):
        return lambda local, grouped, rows: pltpu.make_async_copy(
            y_ref.at[pl.ds(grouped, rows)], buf_ref.at[buf_slot, pl.ds(local, rows)], sem.at[buf_slot])

    @pl.when(step == 0)
    def _():
        buf_ref[...] = jnp.zeros_like(buf_ref)
        _for_each_chunk(seg_ref, n_experts, tm // SEG_ALIGN, lambda *a: segment_copy(slot)(*a).start())

    @pl.when(step < pl.num_programs(0) - 1)
    def _():
        _for_each_chunk(next_seg_ref, n_experts, tm // SEG_ALIGN, lambda *a: segment_copy(1 - slot)(*a).start())

    _for_each_chunk(seg_ref, n_experts, tm // SEG_ALIGN, lambda *a: segment_copy(slot)(*a).wait())

    route = route_ref[...]
    y = buf_ref[slot].astype(BF16)
    col = lax.broadcasted_iota(jnp.int32, (tm, n_local), 1).astype(F32)
    weigh = jnp.zeros((tm, n_local), F32)
    for k in range(2):
        weigh = jnp.where(col == route[:, R_LOCAL + k:R_LOCAL + k + 1], route[:, R_WEIGHT + k:R_WEIGHT + k + 1], weigh)
    x2 = x_ref[...] + _dot(weigh.astype(BF16), y)
    out_ref[...] = _ple_tail(x2, p_ref, gp_ref, wpg_ref, wpp_ref, gf_ref, final)


def _combine(x, route, seg, y, ple, layer, final, tm):
    n, d = x.shape
    n_experts = SEG_LOCAL - SEG_UNITS
    last = n // tm - 1
    seg_spec = lambda shift: pl.BlockSpec((1, 1, seg.shape[2]), lambda i: (jnp.minimum(i + shift, last), 0, 0),
                                          memory_space=pltpu.SMEM)
    return pl.pallas_call(
        functools.partial(_combine_kernel, n_experts=n_experts, final=final),
        grid=(n // tm,),
        in_specs=[seg_spec(0), seg_spec(1),
                  pl.BlockSpec((tm, d), lambda i: (i, 0)),
                  pl.BlockSpec((tm, LANES), lambda i: (i, 0)),
                  pl.BlockSpec(memory_space=pl.ANY)] + _ple_specs(ple, layer, tm),
        out_specs=pl.BlockSpec((tm, d), lambda i: (i, 0)),
        out_shape=jax.ShapeDtypeStruct((n, d), F32),
        scratch_shapes=[pltpu.VMEM((2, _local_rows(tm, n_experts), d), F32), pltpu.SemaphoreType.DMA((2,))],
        compiler_params=_params("arbitrary"),
        name="combine",
    )(seg, seg, x, route, y, *ple)


def _moe(x, g, layer, w_router, w_gu, w_down, idx, ple, final, tm, tmg):
    n = x.shape[0]
    n_experts = w_router.shape[1]
    route, seg, count = _router(x, g, w_router, layer, tm)
    counts = count[0, :n_experts].astype(jnp.int32)
    padded = (counts + tmg - 1) // tmg * tmg
    ends = jnp.cumsum(padded)
    starts = ends - padded
    seg = seg.astype(jnp.int32)[:, :, :4 * SEG_LOCAL]
    seg = seg.at[:, 0, SEG_BASE:SEG_BASE + n_experts].add(starts)
    rows = -(-(2 * n + (n // tm) * n_experts * SEG_ALIGN) // tmg) * tmg + n_experts * tmg
    n_used = (ends[-1:] // tmg).astype(jnp.int32)
    tile_row = jnp.arange(rows // tmg, dtype=jnp.int32) * tmg
    tile_expert = jnp.minimum(jnp.sum(tile_row[:, None] >= ends[None, :], axis=-1), n_experts - 1).astype(jnp.int32)
    pad_start = jnp.concatenate([starts + counts, ends[-1:]]).astype(jnp.int32)

    xs = _dispatch(x, g, layer, route, seg, pad_start, rows, tm, tmg)
    y = _experts(xs, tile_expert, n_used, w_gu, w_down, idx, rows, tmg)
    return _combine(x, route, seg, y, ple, layer, final, tm)


def _tiles(n, seq):
    return min(512, n), min(1024, n), min(512, seq), min(512, n), 512


def _ff_tile(d_ff):
    half = d_ff // 2
    if d_ff % 2 == 0 and half % V7X_MXU_WIDTH == 0:
        return half
    for tf in (512, 256, 128):
        if d_ff % tf == 0:
            return tf
    raise ValueError(f"unsupported hidden width {d_ff}")


def kernel(x, p, positions, norm_mix_g, w_in, conv_w, pool_w, pool_scale, attn_sinks, w_branch, w_out, norm_ffn_g, ffn_w_gu, ffn_w_down, moe_router, moe_w_gu, moe_w_down, norm_ple_g, ple_w_proj, ple_w_gate, final_norm_g):
    batch, seq, d = x.shape
    depth = w_in.shape[0]
    n = batch * seq
    tm_in, tm, tile, tm_row, tmg = _tiles(n, seq)
    assert seq % tile == 0 and tile % WINDOW == 0 and n % tm == 0 and n % tm_in == 0 and n % tm_row == 0
    stack_rows = lambda a: a.reshape(a.shape[0], 1, a.shape[1])

    w_in, pool_w, w_branch, w_out = (a.astype(BF16) for a in (w_in, pool_w, w_branch, w_out))
    ffn_w_gu, ffn_w_down, moe_w_gu, moe_w_down = (a.astype(BF16) for a in (ffn_w_gu, ffn_w_down, moe_w_gu, moe_w_down))
    ple_w_gate, ple_w_proj = ple_w_gate.astype(BF16), ple_w_proj.astype(BF16)
    g_mix, g_ffn, g_ple = stack_rows(norm_mix_g), stack_rows(norm_ffn_g), stack_rows(norm_ple_g)
    pool_scale = stack_rows(pool_scale)
    ple = (p.reshape(depth, n, -1), g_ple, ple_w_gate, ple_w_proj, final_norm_g.reshape(1, d))

    cos, sin = _rope_tables(positions, tm)
    xs = x.reshape(n, d)
    for i in range(depth):
        conv, pq, kv, gates = _in_proj(xs, g_mix, cos, sin, w_in, i, tm_in)
        xs = _mixer(xs, conv, pq, kv, gates, attn_sinks, conv_w, pool_w, pool_scale, w_branch, w_out, i, batch, tile)
        final = i == depth - 1
        if i % 2 == 0:
            xs = _ffn(xs, g_ffn, ffn_w_gu, ffn_w_down, ple, i, i // 2, final, tm_in, _ff_tile(ffn_w_down.shape[1]))
        else:
            xs = _moe(xs, g_ffn, i, moe_router[i // 2], moe_w_gu, moe_w_down, i // 2, ple, final, tm_row, tmg)
    return xs.reshape(batch, seq, d)
```

```python
import functools

import jax
import jax.numpy as jnp
from jax import lax
from jax.experimental import pallas as pl
from jax.experimental.pallas import tpu as pltpu

F32 = jnp.float32
BF16 = jnp.bfloat16

BRANCH_W = 512
N_BRANCH = 3
POOL_WINDOWS = (2, 4, 8, 16)
POOL_GROUP_W = 128
N_HEADS = 8
N_KV_HEADS = 2
HEADS_PER_KV = N_HEADS // N_KV_HEADS
HEAD_DIM = 64
HALF = HEAD_DIM // 2
KV_W = N_KV_HEADS * HEAD_DIM
WINDOW = 128
ROPE_THETA = 10000.0
NEG_INF = -1e30
EPS = 1e-6
OFF_POOL = 3 * BRANCH_W
OFF_Q = OFF_POOL + BRANCH_W
OFF_K = OFF_Q + N_HEADS * HEAD_DIM
OFF_GATE = OFF_K + 2 * KV_W

LANES = 128
V7X_MXU_WIDTH = 256
FFN_CHUNK = 2 * V7X_MXU_WIDTH
V7X_VMEM_LIMIT_BYTES = 56 * 1024 * 1024
HALO = 16


def _params(*sem):
    return pltpu.CompilerParams(dimension_semantics=sem, vmem_limit_bytes=V7X_VMEM_LIMIT_BYTES)


def _rms(x, g):
    ms = jnp.mean(x * x, axis=-1, keepdims=True)
    return x * lax.rsqrt(ms + EPS) * g


def _sigmoid(x):
    return 0.5 * jnp.tanh(0.5 * x) + 0.5


def _dot(a, b):
    return jnp.dot(a, b, preferred_element_type=F32)


def _ple_tail(x, p_ref, g_ref, wg_ref, wp_ref, gf_ref, final):
    h = _rms(x, g_ref[...]).astype(BF16)
    y = x + _sigmoid(_dot(h, wg_ref[...])) * _dot(p_ref[...].astype(BF16), wp_ref[...])
    return _rms(y, gf_ref[...]) if final else y


def _ple_specs(ple, layer, tm):
    p, g, w_gate, w_proj, g_final = ple
    return [pl.BlockSpec((None, tm, p.shape[2]), lambda i, *_: (layer, i, 0)), _layer_row(g, layer),
            _layer_full(w_gate, layer), _layer_full(w_proj, layer),
            pl.BlockSpec(g_final.shape, lambda *_: (0, 0))]


def _layer_row(stacked, layer):
    return pl.BlockSpec((None, 1, stacked.shape[2]), lambda *_: (layer, 0, 0))


def _layer_full(stacked, layer):
    shape = stacked.shape[1:]
    return pl.BlockSpec((None,) + shape, lambda *_: (layer,) + (0,) * len(shape))


def _rope_kernel(pos_ref, freq_ref, sign_ref, cos_ref, sin_ref):
    ang = pos_ref[...].astype(F32) * freq_ref[...]
    cos_ref[...] = jnp.cos(ang)
    sin_ref[...] = jnp.sin(ang) * sign_ref[...]


def _rope_tables(positions, tm):
    n = positions.size
    inv_freq = ROPE_THETA ** (-2.0 * jnp.arange(HALF, dtype=F32) / HEAD_DIM)
    freq = jnp.tile(inv_freq, LANES // HALF).reshape(1, LANES)
    sign = jnp.tile(jnp.concatenate([-jnp.ones(HALF, F32), jnp.ones(HALF, F32)]), LANES // HEAD_DIM).reshape(1, LANES)
    row = pl.BlockSpec((1, LANES), lambda i: (0, 0))
    tab = pl.BlockSpec((tm, LANES), lambda i: (i, 0))
    return pl.pallas_call(
        _rope_kernel,
        grid=(n // tm,),
        in_specs=[pl.BlockSpec((tm, 1), lambda i: (i, 0)), row, row],
        out_specs=[tab, tab],
        out_shape=[jax.ShapeDtypeStruct((n, LANES), F32)] * 2,
        compiler_params=_params("parallel"),
        name="rope_tables",
    )(positions.reshape(n, 1), freq, sign)


def _rotate_half(x, cos, sin_signed):
    width = x.shape[-1]
    lane = lax.broadcasted_iota(jnp.int32, x.shape, 1)
    first = (lane % HEAD_DIM) < HALF
    partner = jnp.where(first, pltpu.roll(x, width - HALF, 1), pltpu.roll(x, HALF, 1))
    return x * cos + partner * sin_signed


def _in_proj_kernel(x_ref, g_ref, cos_ref, sin_ref, w_ref, conv_ref, pq_ref, kv_ref, gate_ref):
    h = _rms(x_ref[...], g_ref[...]).astype(BF16)

    def proj(c0, width):
        return _dot(h, w_ref[:, c0:c0 + width])

    conv_ref[:, :BRANCH_W] = proj(0, BRANCH_W).astype(BF16)
    conv_ref[:, BRANCH_W:] = (proj(BRANCH_W, BRANCH_W) * proj(2 * BRANCH_W, BRANCH_W)).astype(BF16)
    pq_ref[:, :BRANCH_W] = proj(OFF_POOL, BRANCH_W).astype(BF16)
    cos = cos_ref[...]
    sin = sin_ref[...]
    rep = N_HEADS * HEAD_DIM // LANES
    q = _rotate_half(proj(OFF_Q, N_HEADS * HEAD_DIM), jnp.concatenate([cos] * rep, axis=1),
                     jnp.concatenate([sin] * rep, axis=1))
    pq_ref[:, BRANCH_W:] = (q * HEAD_DIM ** -0.5).astype(BF16)
    kv = proj(OFF_K, 2 * KV_W)
    kv_ref[:, :KV_W] = _rotate_half(kv[:, :KV_W], cos, sin).astype(BF16)
    kv_ref[:, KV_W:] = kv[:, KV_W:].astype(BF16)
    d = gate_ref.shape[1] // N_BRANCH
    for r in range(N_BRANCH):
        gate_ref[:, r * d:(r + 1) * d] = _sigmoid(proj(OFF_GATE + r * d, d)).astype(BF16)


def _in_proj(x, g, cos, sin, w_in, layer, tm):
    n, d = x.shape
    d_in = w_in.shape[2]
    tok = lambda width: pl.BlockSpec((tm, width), lambda i: (i, 0))
    widths = (2 * BRANCH_W, OFF_Q + N_HEADS * HEAD_DIM - OFF_POOL, 2 * KV_W, N_BRANCH * d)
    return pl.pallas_call(
        _in_proj_kernel,
        grid=(n // tm,),
        in_specs=[tok(d), _layer_row(g, layer), tok(LANES), tok(LANES),
                  pl.BlockSpec((None, d, d_in), lambda i: (layer, 0, 0), pipeline_mode=pl.Buffered(1))],
        out_specs=[tok(w) for w in widths],
        out_shape=[jax.ShapeDtypeStruct((n, w), BF16) for w in widths],
        compiler_params=_params("parallel"),
        name="in_proj",
    )(x, g, cos, sin, w_in)


def _mixer_kernel(sinks_ref, conv_ref, convh_ref, pq_ref, poolh_ref, kv_ref, kvh_ref, gate_ref, x_ref,
                  convw_ref, poolw_ref, pscale_ref, wbr_ref, wout_ref, out_ref, att_ref, *, tile, layer):
    seq_tile = pl.program_id(1)
    not_first = seq_tile > 0
    w = BRANCH_W

    v = conv_ref[:, w:].astype(F32)
    vh = convh_ref[:, w:].astype(F32)
    ext = jnp.concatenate([jnp.where(not_first, vh, 0.0), v], axis=0)
    cw = convw_ref[...]
    y = cw[2:3] * v + cw[1:2] * pltpu.roll(ext, 1, 0)[HALO:] + cw[0:1] * pltpu.roll(ext, 2, 0)[HALO:]
    y_conv = (conv_ref[:, :w].astype(F32) * y).astype(BF16)

    pu = pq_ref[:, :w].astype(F32)
    s = jnp.concatenate([jnp.where(not_first, poolh_ref[...].astype(F32), 0.0), pu], axis=0)
    pos1 = seq_tile * tile + lax.broadcasted_iota(jnp.int32, (tile, 1), 0) + 1
    mixed = []
    for gi, win in enumerate(POOL_WINDOWS):
        s = s[:, (POOL_GROUP_W if gi else 0):]
        s = s + pltpu.roll(s, win // 2, 0)
        inv_count = 1.0 / jnp.minimum(pos1, win).astype(F32)
        lo_c = gi * POOL_GROUP_W
        pooled = s[HALO:, :POOL_GROUP_W] * inv_count - pu[:, lo_c:lo_c + POOL_GROUP_W]
        mixed.append(_dot(pooled.astype(BF16), poolw_ref[gi]))
    y_pool = (jnp.concatenate(mixed, axis=1) * pscale_ref[...]).astype(BF16)

    d = x_ref.shape[1]
    gated = lambda r, br: gate_ref[:, r * d:(r + 1) * d].astype(F32) * _dot(br, wbr_ref[r])
    early = {0: y_conv, 1: y_pool}
    merged = []

    kvh = jnp.where(not_first, kvh_ref[...], jnp.zeros_like(kvh_ref[...]))
    kext = jnp.concatenate([kvh, kv_ref[...]], axis=0)
    qw = HEADS_PER_KV * HEAD_DIM
    rows = HEADS_PER_KV * WINDOW
    lo = lax.broadcasted_iota(jnp.int32, (2 * WINDOW, KV_W), 1) < HEAD_DIM
    lane_head = lax.broadcasted_iota(jnp.int32, (WINDOW, qw), 1) // HEAD_DIM
    qi = lax.broadcasted_iota(jnp.int32, (rows, 2 * WINDOW), 0) % WINDOW
    key = lax.broadcasted_iota(jnp.int32, (rows, 2 * WINDOW), 1)
    dist = qi + WINDOW - key
    in_band = (dist >= 0) & (dist < WINDOW)
    row_head = lax.broadcasted_iota(jnp.int32, (rows, 1), 0) // WINDOW
    for j in range(tile // WINDOW):
        band = kext[j * WINDOW:(j + 2) * WINDOW].astype(F32)
        halves = [(t, pltpu.roll(t, HEAD_DIM, 1)) for t in (band[:, :KV_W], band[:, KV_W:])]
        valid = in_band if j else in_band & ((key >= WINDOW) | not_first)
        for g in range(N_KV_HEADS):
            k_rep, v_rep = (
                jnp.concatenate([jnp.where(lo, r, t) if g else jnp.where(lo, t, r)] * (qw // KV_W), axis=1).astype(BF16)
                for t, r in halves)
            qg = pq_ref[j * WINDOW:(j + 1) * WINDOW, w + g * qw:w + (g + 1) * qw]
            q_heads = jnp.concatenate(
                [jnp.where(lane_head == hh, qg, jnp.zeros_like(qg)) for hh in range(HEADS_PER_KV)], axis=0)
            scores = lax.dot_general(q_heads, k_rep, (((1,), (1,)), ((), ())), preferred_element_type=F32)
            sink = jnp.zeros((rows, 1), F32)
            for hh in range(HEADS_PER_KV):
                sink = jnp.where(row_head == hh, sinks_ref[layer, g * HEADS_PER_KV + hh], sink)
            sh = jnp.where(valid, scores, NEG_INF)
            m = jnp.maximum(jnp.max(sh, axis=-1, keepdims=True), sink)
            pexp = jnp.exp(sh - m)
            denom = jnp.sum(pexp, axis=-1, keepdims=True) + jnp.exp(sink - m)
            res = _dot((pexp * (1.0 / denom)).astype(BF16), v_rep)
            out = res[:WINDOW]
            for hh in range(1, HEADS_PER_KV):
                out = jnp.where(lane_head == hh, res[hh * WINDOW:(hh + 1) * WINDOW], out)
            att_ref[j * WINDOW:(j + 1) * WINDOW, g * qw:(g + 1) * qw] = out.astype(BF16)
        if j in early:
            merged.append(gated(j, early[j]))

    merged = merged[0] + merged[1] + gated(2, att_ref[...])
    out_ref[...] = x_ref[...] + _dot(merged.astype(BF16), wout_ref[...])


def _mixer(x, conv, pq, kv, gates, sinks, conv_w, pool_w, pool_scale, w_branch, w_out, layer, batch, tile):
    n, d = x.shape
    n_t = n // batch // tile
    t_idx = lambda b, i: b * n_t + i
    tok = lambda width: pl.BlockSpec((tile, width), lambda b, i: (t_idx(b, i), 0))
    prev = lambda rows, width: pl.BlockSpec(
        (rows, width), lambda b, i: (jnp.maximum(t_idx(b, i) * (tile // rows) - 1, 0), 0))
    return pl.pallas_call(
        functools.partial(_mixer_kernel, tile=tile, layer=layer),
        grid=(batch, n_t),
        in_specs=[pl.BlockSpec(memory_space=pltpu.SMEM),
                  tok(conv.shape[1]), prev(HALO, conv.shape[1]),
                  tok(pq.shape[1]), prev(HALO, BRANCH_W),
                  tok(kv.shape[1]), prev(WINDOW, kv.shape[1]),
                  tok(gates.shape[1]), tok(d),
                  _layer_full(conv_w, layer), _layer_full(pool_w, layer), _layer_row(pool_scale, layer),
                  _layer_full(w_branch, layer), _layer_full(w_out, layer)],
        out_specs=tok(d),
        out_shape=jax.ShapeDtypeStruct((n, d), F32),
        scratch_shapes=[pltpu.VMEM((tile, N_HEADS * HEAD_DIM), BF16)],
        compiler_params=_params("parallel", "parallel"),
        name="mixer",
    )(sinks, conv, conv, pq, pq, kv, kv, gates, x, conv_w, pool_w, pool_scale, w_branch, w_out)


def _ffn_kernel(x_ref, g_ref, wg_ref, wu_ref, wd_ref, p_ref, gp_ref, wpg_ref, wpp_ref, gf_ref, out_ref, h_ref, acc_ref,
                *, final):
    j = pl.program_id(1)

    @pl.when(j == 0)
    def _():
        h_ref[...] = _rms(x_ref[...], g_ref[...]).astype(BF16)
        acc_ref[...] = jnp.zeros_like(acc_ref)

    h = h_ref[...]
    acts = []
    tf = wg_ref.shape[1]
    for c in range(0, tf, FFN_CHUNK):
        cols = slice(c, min(c + FFN_CHUNK, tf))
        hg = _dot(h, wg_ref[:, cols])
        acts.append((hg * _sigmoid(hg) * _dot(h, wu_ref[:, cols])).astype(BF16))
    acc_ref[...] += _dot(jnp.concatenate(acts, axis=1), wd_ref[...])

    @pl.when(j == pl.num_programs(1) - 1)
    def _():
        out_ref[...] = _ple_tail(x_ref[...] + acc_ref[...], p_ref, gp_ref, wpg_ref, wpp_ref, gf_ref, final)


def _ffn(x, g, w_gu, w_down, ple, layer, idx, final, tm, tf):
    n, d = x.shape
    n_f = w_down.shape[1] // tf
    mode = dict(pipeline_mode=pl.Buffered(1)) if n_f == 1 else {}
    return pl.pallas_call(
        functools.partial(_ffn_kernel, final=final),
        grid=(n // tm, n_f),
        in_specs=[pl.BlockSpec((tm, d), lambda i, j: (i, 0)),
                  _layer_row(g, layer),
                  pl.BlockSpec((None, d, tf), lambda i, j: (idx, 0, j), **mode),
                  pl.BlockSpec((None, d, tf), lambda i, j: (idx, 0, j + n_f), **mode),
                  pl.BlockSpec((None, tf, d), lambda i, j: (idx, j, 0), **mode)] + _ple_specs(ple, layer, tm),
        out_specs=pl.BlockSpec((tm, d), lambda i, j: (i, 0)),
        out_shape=jax.ShapeDtypeStruct((n, d), F32),
        scratch_shapes=[pltpu.VMEM((tm, d), BF16), pltpu.VMEM((tm, d), F32)],
        compiler_params=_params("parallel", "arbitrary"),
        name="ffn",
    )(x, g, w_gu, w_gu, w_down, *ple)


SEG_ALIGN = 8
R_LOCAL, R_WEIGHT = 2, 4
SEG_UNITS, SEG_LOCAL, SEG_BASE = 0, 8, 16


def _local_rows(tm, n_experts):
    return -(-(2 * tm + SEG_ALIGN * n_experts) // LANES) * LANES


def _router_kernel(x_ref, g_ref, wr_ref, route_ref, seg_ref, count_ref, base_ref, *, n_experts):
    @pl.when(pl.program_id(0) == 0)
    def _():
        base_ref[...] = jnp.zeros_like(base_ref)

    tm = x_ref.shape[0]
    h = _rms(x_ref[...], g_ref[...]).astype(BF16)
    lane_i = lax.broadcasted_iota(jnp.int32, (tm, LANES), 1)
    lane = lane_i.astype(F32)
    logits = jnp.where(lane_i < n_experts, _dot(h, wr_ref[...]), NEG_INF)

    def take_max(vals):
        top = jnp.max(vals, axis=-1, keepdims=True)
        idx = jnp.min(jnp.where(vals == top, lane, float(LANES)), axis=-1, keepdims=True)
        return top, idx

    v1, i1 = take_max(logits)
    v2, i2 = take_max(jnp.where(lane == i1, NEG_INF, logits))
    e2 = jnp.exp(v2 - v1)
    w1 = 1.0 / (1.0 + e2)
    hit1, hit2 = lane == i1, lane == i2
    onehot = jnp.where(hit1 | hit2, 1.0, 0.0)
    earlier = (lax.broadcasted_iota(jnp.int32, (tm, tm), 0) > lax.broadcasted_iota(jnp.int32, (tm, tm), 1))
    before = _dot(earlier.astype(BF16), onehot.astype(BF16))
    units = jnp.ceil(jnp.sum(onehot, axis=0, keepdims=True) * (1.0 / SEG_ALIGN))
    lower = (lax.broadcasted_iota(jnp.int32, (LANES, LANES), 0) < lax.broadcasted_iota(jnp.int32, (LANES, LANES), 1))
    local = SEG_ALIGN * _dot(jnp.broadcast_to(units, (SEG_ALIGN, LANES)).astype(BF16), lower.astype(BF16))[:1]
    row = before + local
    l1 = jnp.sum(jnp.where(hit1, row, 0.0), axis=-1, keepdims=True)
    l2 = jnp.sum(jnp.where(hit2, row, 0.0), axis=-1, keepdims=True)
    route = jnp.zeros((tm, LANES), F32)
    for k, val in enumerate((i1, i2, l1, l2, w1, e2 * w1)):
        route = jnp.where(lane_i == k, val, route)
    route_ref[...] = route
    roll_to = lambda a, k: pltpu.roll(a, k, 1)
    first = lane_i[:1] < n_experts
    seg = (jnp.where(first, units, 0.0) + roll_to(jnp.where(first, local, 0.0), SEG_LOCAL)
           + roll_to(jnp.where(first, base_ref[...], 0.0), SEG_BASE))
    seg_ref[...] = seg
    base_ref[...] += SEG_ALIGN * units
    count_ref[...] = base_ref[...]


def _router(x, g, w_router, layer, tm):
    n, d = x.shape
    n_experts = w_router.shape[1]
    assert n_experts <= SEG_LOCAL
    wr = jnp.pad(w_router, ((0, 0), (0, LANES - n_experts))).astype(BF16)
    return pl.pallas_call(
        functools.partial(_router_kernel, n_experts=n_experts),
        grid=(n // tm,),
        in_specs=[pl.BlockSpec((tm, d), lambda i: (i, 0)), _layer_row(g, layer),
                  pl.BlockSpec((d, LANES), lambda i: (0, 0))],
        out_specs=[pl.BlockSpec((tm, LANES), lambda i: (i, 0)),
                   pl.BlockSpec((None, 1, LANES), lambda i: (i, 0, 0)),
                   pl.BlockSpec((1, LANES), lambda i: (0, 0))],
        out_shape=[jax.ShapeDtypeStruct((n, LANES), F32), jax.ShapeDtypeStruct((n // tm, 1, LANES), F32),
                   jax.ShapeDtypeStruct((1, LANES), F32)],
        scratch_shapes=[pltpu.VMEM((1, LANES), F32)],
        compiler_params=_params("arbitrary"),
        name="router",
    )(x, g, wr)


def _for_each_chunk(seg_ref, n_experts, max_units, fn):
    for e in range(n_experts):
        units = seg_ref[0, 0, SEG_UNITS + e]
        local = seg_ref[0, 0, SEG_LOCAL + e]
        grouped = seg_ref[0, 0, SEG_BASE + e]
        for b in range(max_units.bit_length()):
            skip = (units >> (b + 1) << (b + 1)) * SEG_ALIGN

            @pl.when((units >> b) & 1 == 1)
            def _():
                fn(pl.multiple_of(local + skip, SEG_ALIGN), pl.multiple_of(grouped + skip, SEG_ALIGN), SEG_ALIGN << b)


def _dispatch_kernel(pad_ref, seg_ref, prev_seg_ref, x_ref, g_ref, route_ref, xs_ref, sorted_ref, zero_ref, sem,
                     *, n_experts, n_tokens):
    step = pl.program_id(0)
    tm = x_ref.shape[0]
    tmg = zero_ref.shape[0]
    max_tail = (xs_ref.shape[0] - 2 * n_tokens) // tmg
    fill = lambda row: pltpu.make_async_copy(zero_ref, xs_ref.at[pl.ds(pl.multiple_of(row, SEG_ALIGN), tmg)], sem.at[0])

    @pl.when(step == 0)
    def _():
        zero_ref[...] = jnp.zeros_like(zero_ref)
        for e in range(n_experts):
            fill(pad_ref[e]).start()
        for e in range(n_experts):
            fill(pad_ref[e]).wait()
        for t in range(max_tail):
            row = pad_ref[n_experts] + t * tmg

            @pl.when(row < xs_ref.shape[0])
            def _():
                fill(row).start()
                fill(row).wait()

    slot = step % 2
    n_local = sorted_ref.shape[1]
    h = _rms(x_ref[...], g_ref[...]).astype(BF16)
    owner = route_ref[...].T
    row = lax.broadcasted_iota(jnp.int32, (n_local, tm), 0).astype(F32)
    pick = (row == owner[R_LOCAL:R_LOCAL + 1]) | (row == owner[R_LOCAL + 1:R_LOCAL + 2])
    sorted_ref[slot] = _dot(jnp.where(pick, 1.0, 0.0).astype(BF16), h)

    def segment_copy(buf_slot):
        return lambda local, grouped, rows: pltpu.make_async_copy(
            sorted_ref.at[buf_slot, pl.ds(local, rows)], xs_ref.at[pl.ds(grouped, rows)], sem.at[buf_slot])

    _for_each_chunk(seg_ref, n_experts, tm // SEG_ALIGN, lambda *a: segment_copy(slot)(*a).start())

    @pl.when(step > 0)
    def _():
        _for_each_chunk(prev_seg_ref, n_experts, tm // SEG_ALIGN, lambda *a: segment_copy(1 - slot)(*a).wait())

    @pl.when(step == pl.num_programs(0) - 1)
    def _():
        _for_each_chunk(seg_ref, n_experts, tm // SEG_ALIGN, lambda *a: segment_copy(slot)(*a).wait())


def _dispatch(x, g, layer, route, seg, pad_start, rows, tm, tmg):
    n, d = x.shape
    n_experts = pad_start.shape[0] - 1
    seg_spec = lambda shift: pl.BlockSpec((1, 1, seg.shape[2]), lambda i, pad: (jnp.maximum(i - shift, 0), 0, 0),
                                          memory_space=pltpu.SMEM)
    return pl.pallas_call(
        functools.partial(_dispatch_kernel, n_experts=n_experts, n_tokens=n),
        grid_spec=pltpu.PrefetchScalarGridSpec(
            num_scalar_prefetch=1,
            grid=(n // tm,),
            in_specs=[seg_spec(0), seg_spec(1),
                      pl.BlockSpec((tm, d), lambda i, pad: (i, 0)),
                      _layer_row(g, layer),
                      pl.BlockSpec((tm, LANES), lambda i, pad: (i, 0))],
            out_specs=pl.BlockSpec(memory_space=pl.ANY),
            scratch_shapes=[pltpu.VMEM((2, _local_rows(tm, n_experts), d), F32), pltpu.VMEM((tmg, d), F32),
                            pltpu.SemaphoreType.DMA((2,))],
        ),
        out_shape=jax.ShapeDtypeStruct((rows + tmg, d), F32),
        compiler_params=_params("arbitrary"),
        name="dispatch",
    )(pad_start, seg, seg, x, g, route)


def _expert_kernel(te_ref, used_ref, xs_ref, wg_ref, wu_ref, wd_ref, y_ref):
    @pl.when(pl.program_id(0) < used_ref[0])
    def _():
        h = xs_ref[...].astype(BF16)
        hg = _dot(h, wg_ref[0])
        act = (hg * _sigmoid(hg) * _dot(h, wu_ref[0])).astype(BF16)
        y_ref[...] = _dot(act, wd_ref[0])

    @pl.when(pl.program_id(0) >= used_ref[0])
    def _():
        y_ref[...] = jnp.zeros_like(y_ref)


def _experts(xs, tile_expert, n_used, w_gu, w_down, idx, rows, tmg):
    d = xs.shape[1]
    d_ff = w_down.shape[2]
    live = lambda t, te, used: jnp.minimum(t, used[0] - 1)
    return pl.pallas_call(
        _expert_kernel,
        grid_spec=pltpu.PrefetchScalarGridSpec(
            num_scalar_prefetch=2,
            grid=(rows // tmg,),
            in_specs=[pl.BlockSpec((tmg, d), lambda t, te, used: (live(t, te, used), 0)),
                      pl.BlockSpec((None, 1, d, d_ff), lambda t, te, used: (idx, te[live(t, te, used)], 0, 0)),
                      pl.BlockSpec((None, 1, d, d_ff), lambda t, te, used: (idx, te[live(t, te, used)], 0, 1)),
                      pl.BlockSpec((None, 1, d_ff, d), lambda t, te, used: (idx, te[live(t, te, used)], 0, 0))],
            out_specs=pl.BlockSpec((tmg, d), lambda t, te, used: (t, 0)),
        ),
        out_shape=jax.ShapeDtypeStruct((rows, d), F32),
        compiler_params=_params("arbitrary"),
        name="experts",
    )(tile_expert, n_used, xs, w_gu, w_gu, w_down)


def _combine_kernel(seg_ref, next_seg_ref, x_ref, route_ref, y_ref, p_ref, gp_ref, wpg_ref, wpp_ref, gf_ref,
                    out_ref, buf_ref, sem, *, n_experts, final):
    step = pl.program_id(0)
    tm = x_ref.shape[0]
    slot = step % 2
    n_local = buf_ref.shape[1]

    def segment_copy(buf_slot):
        return lambda local, grouped, rows: pltpu.make_async_copy(
            y_ref.at[pl.ds(grouped, rows)], buf_ref.at[buf_slot, pl.ds(local, rows)], sem.at[buf_slot])

    @pl.when(step == 0)
    def _():
        buf_ref[...] = jnp.zeros_like(buf_ref)
        _for_each_chunk(seg_ref, n_experts, tm // SEG_ALIGN, lambda *a: segment_copy(slot)(*a).start())

    @pl.when(step < pl.num_programs(0) - 1)
    def _():
        _for_each_chunk(next_seg_ref, n_experts, tm // SEG_ALIGN, lambda *a: segment_copy(1 - slot)(*a).start())

    _for_each_chunk(seg_ref, n_experts, tm // SEG_ALIGN, lambda *a: segment_copy(slot)(*a).wait())

    route = route_ref[...]
    y = buf_ref[slot].astype(BF16)
    col = lax.broadcasted_iota(jnp.int32, (tm, n_local), 1).astype(F32)
    weigh = jnp.zeros((tm, n_local), F32)
    for k in range(2):
        weigh = jnp.where(col == route[:, R_LOCAL + k:R_LOCAL + k + 1], route[:, R_WEIGHT + k:R_WEIGHT + k + 1], weigh)
    x2 = x_ref[...] + _dot(weigh.astype(BF16), y)
    out_ref[...] = _ple_tail(x2, p_ref, gp_ref, wpg_ref, wpp_ref, gf_ref, final)


def _combine(x, route, seg, y, ple, layer, final, tm):
    n, d = x.shape
    n_experts = SEG_LOCAL - SEG_UNITS
    last = n // tm - 1
    seg_spec = lambda shift: pl.BlockSpec((1, 1, seg.shape[2]), lambda i: (jnp.minimum(i + shift, last), 0, 0),
                                          memory_space=pltpu.SMEM)
    return pl.pallas_call(
        functools.partial(_combine_kernel, n_experts=n_experts, final=final),
        grid=(n // tm,),
        in_specs=[seg_spec(0), seg_spec(1),
                  pl.BlockSpec((tm, d), lambda i: (i, 0)),
                  pl.BlockSpec((tm, LANES), lambda i: (i, 0)),
                  pl.BlockSpec(memory_space=pl.ANY)] + _ple_specs(ple, layer, tm),
        out_specs=pl.BlockSpec((tm, d), lambda i: (i, 0)),
        out_shape=jax.ShapeDtypeStruct((n, d), F32),
        scratch_shapes=[pltpu.VMEM((2, _local_rows(tm, n_experts), d), F32), pltpu.SemaphoreType.DMA((2,))],
        compiler_params=_params("arbitrary"),
        name="combine",
    )(seg, seg, x, route, y, *ple)


def _moe(x, g, layer, w_router, w_gu, w_down, idx, ple, final, tm, tmg):
    n = x.shape[0]
    n_experts = w_router.shape[1]
    route, seg, count = _router(x, g, w_router, layer, tm)
    counts = count[0, :n_experts].astype(jnp.int32)
    padded = (counts + tmg - 1) // tmg * tmg
    ends = jnp.cumsum(padded)
    starts = ends - padded
    seg = seg.astype(jnp.int32)[:, :, :4 * SEG_LOCAL]
    seg = seg.at[:, 0, SEG_BASE:SEG_BASE + n_experts].add(starts)
    rows = -(-(2 * n + (n // tm) * n_experts * SEG_ALIGN) // tmg) * tmg + n_experts * tmg
    n_used = (ends[-1:] // tmg).astype(jnp.int32)
    tile_row = jnp.arange(rows // tmg, dtype=jnp.int32) * tmg
    tile_expert = jnp.minimum(jnp.sum(tile_row[:, None] >= ends[None, :], axis=-1), n_experts - 1).astype(jnp.int32)
    pad_start = jnp.concatenate([starts + counts, ends[-1:]]).astype(jnp.int32)

    xs = _dispatch(x, g, layer, route, seg, pad_start, rows, tm, tmg)
    y = _experts(xs, tile_expert, n_used, w_gu, w_down, idx, rows, tmg)
    return _combine(x, route, seg, y, ple, layer, final, tm)


def _tiles(n, seq):
    return min(512, n), min(1024, n), min(512, seq), min(512, n), 512


def _ff_tile(d, d_ff):
    if 3 * d * d_ff * 2 <= V7X_VMEM_LIMIT_BYTES // 2:
        return d_ff
    half = d_ff // 2
    if d_ff % 2 == 0 and half % V7X_MXU_WIDTH == 0:
        return half
    for tf in (512, 256, 128):
        if d_ff % tf == 0:
            return tf
    raise ValueError(f"unsupported hidden width {d_ff}")


def kernel(x, p, positions, norm_mix_g, w_in, conv_w, pool_w, pool_scale, attn_sinks, w_branch, w_out, norm_ffn_g, ffn_w_gu, ffn_w_down, moe_router, moe_w_gu, moe_w_down, norm_ple_g, ple_w_proj, ple_w_gate, final_norm_g):
    batch, seq, d = x.shape
    depth = w_in.shape[0]
    n = batch * seq
    tm_in, tm, tile, tm_row, tmg = _tiles(n, seq)
    assert seq % tile == 0 and tile % WINDOW == 0 and n % tm == 0 and n % tm_in == 0 and n % tm_row == 0
    stack_rows = lambda a: a.reshape(a.shape[0], 1, a.shape[1])

    w_in, pool_w, w_branch, w_out = (a.astype(BF16) for a in (w_in, pool_w, w_branch, w_out))
    ffn_w_gu, ffn_w_down, moe_w_gu, moe_w_down = (a.astype(BF16) for a in (ffn_w_gu, ffn_w_down, moe_w_gu, moe_w_down))
    ple_w_gate, ple_w_proj = ple_w_gate.astype(BF16), ple_w_proj.astype(BF16)
    g_mix, g_ffn, g_ple = stack_rows(norm_mix_g), stack_rows(norm_ffn_g), stack_rows(norm_ple_g)
    pool_scale = stack_rows(pool_scale)
    ple = (p.reshape(depth, n, -1), g_ple, ple_w_gate, ple_w_proj, final_norm_g.reshape(1, d))

    cos, sin = _rope_tables(positions, tm)
    xs = x.reshape(n, d)
    for i in range(depth):
        conv, pq, kv, gates = _in_proj(xs, g_mix, cos, sin, w_in, i, tm_in)
        xs = _mixer(xs, conv, pq, kv, gates, attn_sinks, conv_w, pool_w, pool_scale, w_branch, w_out, i, batch, tile)
        final = i == depth - 1
        if i % 2 == 0:
            xs = _ffn(xs, g_ffn, ffn_w_gu, ffn_w_down, ple, i, i // 2, final, tm_in, _ff_tile(d, ffn_w_down.shape[1]))
        else:
            xs = _moe(xs, g_ffn, i, moe_router[i // 2], moe_w_gu, moe_w_down, i // 2, ple, final, tm_row, tmg)
    return xs.reshape(batch, seq, d)
```

```python
import functools

import jax
import jax.numpy as jnp
from jax import lax
from jax.experimental import pallas as pl
from jax.experimental.pallas import tpu as pltpu

F32 = jnp.float32
BF16 = jnp.bfloat16

BRANCH_W = 512
N_BRANCH = 3
POOL_WINDOWS = (2, 4, 8, 16)
POOL_GROUP_W = 128
N_HEADS = 8
N_KV_HEADS = 2
HEADS_PER_KV = N_HEADS // N_KV_HEADS
HEAD_DIM = 64
HALF = HEAD_DIM // 2
KV_W = N_KV_HEADS * HEAD_DIM
WINDOW = 128
ROPE_THETA = 10000.0
NEG_INF = -1e30
EPS = 1e-6
OFF_POOL = 3 * BRANCH_W
OFF_Q = OFF_POOL + BRANCH_W
OFF_K = OFF_Q + N_HEADS * HEAD_DIM
OFF_GATE = OFF_K + 2 * KV_W

LANES = 128
V7X_MXU_WIDTH = 256
FFN_CHUNK = 2 * V7X_MXU_WIDTH
V7X_VMEM_LIMIT_BYTES = 56 * 1024 * 1024
HALO = 16


def _params(*sem):
    return pltpu.CompilerParams(dimension_semantics=sem, vmem_limit_bytes=V7X_VMEM_LIMIT_BYTES)


def _rms(x, g):
    ms = jnp.mean(x * x, axis=-1, keepdims=True)
    return x * lax.rsqrt(ms + EPS) * g


def _sigmoid(x):
    return 0.5 * jnp.tanh(0.5 * x) + 0.5


def _dot(a, b):
    return jnp.dot(a, b, preferred_element_type=F32)


def _ple_tail(x, p_ref, g_ref, wg_ref, wp_ref, gf_ref, final):
    h = _rms(x, g_ref[...]).astype(BF16)
    y = x + _sigmoid(_dot(h, wg_ref[...])) * _dot(p_ref[...].astype(BF16), wp_ref[...])
    return _rms(y, gf_ref[...]) if final else y


def _ple_specs(ple, layer, tm):
    p, g, w_gate, w_proj, g_final = ple
    return [pl.BlockSpec((None, tm, p.shape[2]), lambda i, *_: (layer, i, 0)), _layer_row(g, layer),
            _layer_full(w_gate, layer), _layer_full(w_proj, layer),
            pl.BlockSpec(g_final.shape, lambda *_: (0, 0))]


def _layer_row(stacked, layer):
    return pl.BlockSpec((None, 1, stacked.shape[2]), lambda *_: (layer, 0, 0))


def _layer_full(stacked, layer):
    shape = stacked.shape[1:]
    return pl.BlockSpec((None,) + shape, lambda *_: (layer,) + (0,) * len(shape))


def _rope_kernel(pos_ref, freq_ref, sign_ref, cos_ref, sin_ref):
    ang = pos_ref[...].astype(F32) * freq_ref[...]
    cos_ref[...] = jnp.cos(ang)
    sin_ref[...] = jnp.sin(ang) * sign_ref[...]


def _rope_tables(positions, tm):
    n = positions.size
    inv_freq = ROPE_THETA ** (-2.0 * jnp.arange(HALF, dtype=F32) / HEAD_DIM)
    freq = jnp.tile(inv_freq, LANES // HALF).reshape(1, LANES)
    sign = jnp.tile(jnp.concatenate([-jnp.ones(HALF, F32), jnp.ones(HALF, F32)]), LANES // HEAD_DIM).reshape(1, LANES)
    row = pl.BlockSpec((1, LANES), lambda i: (0, 0))
    tab = pl.BlockSpec((tm, LANES), lambda i: (i, 0))
    return pl.pallas_call(
        _rope_kernel,
        grid=(n // tm,),
        in_specs=[pl.BlockSpec((tm, 1), lambda i: (i, 0)), row, row],
        out_specs=[tab, tab],
        out_shape=[jax.ShapeDtypeStruct((n, LANES), F32)] * 2,
        compiler_params=_params("parallel"),
        name="rope_tables",
    )(positions.reshape(n, 1), freq, sign)


def _rotate_half(x, cos, sin_signed):
    width = x.shape[-1]
    lane = lax.broadcasted_iota(jnp.int32, x.shape, 1)
    first = (lane % HEAD_DIM) < HALF
    partner = jnp.where(first, pltpu.roll(x, width - HALF, 1), pltpu.roll(x, HALF, 1))
    return x * cos + partner * sin_signed


def _in_proj_kernel(x_ref, g_ref, cos_ref, sin_ref, w_ref, conv_ref, pq_ref, kv_ref, gate_ref):
    h = _rms(x_ref[...], g_ref[...]).astype(BF16)

    def proj(c0, width):
        return _dot(h, w_ref[:, c0:c0 + width])

    conv_ref[:, :BRANCH_W] = proj(0, BRANCH_W).astype(BF16)
    conv_ref[:, BRANCH_W:] = (proj(BRANCH_W, BRANCH_W) * proj(2 * BRANCH_W, BRANCH_W)).astype(BF16)
    pq_ref[:, :BRANCH_W] = proj(OFF_POOL, BRANCH_W).astype(BF16)
    cos = cos_ref[...]
    sin = sin_ref[...]
    rep = N_HEADS * HEAD_DIM // LANES
    q = _rotate_half(proj(OFF_Q, N_HEADS * HEAD_DIM), jnp.concatenate([cos] * rep, axis=1),
                     jnp.concatenate([sin] * rep, axis=1))
    pq_ref[:, BRANCH_W:] = (q * HEAD_DIM ** -0.5).astype(BF16)
    kv = proj(OFF_K, 2 * KV_W)
    kv_ref[:, :KV_W] = _rotate_half(kv[:, :KV_W], cos, sin).astype(BF16)
    kv_ref[:, KV_W:] = kv[:, KV_W:].astype(BF16)
    d = gate_ref.shape[1] // N_BRANCH
    for r in range(N_BRANCH):
        gate_ref[:, r * d:(r + 1) * d] = _sigmoid(proj(OFF_GATE + r * d, d)).astype(BF16)


def _in_proj(x, g, cos, sin, w_in, layer, tm):
    n, d = x.shape
    d_in = w_in.shape[2]
    tok = lambda width: pl.BlockSpec((tm, width), lambda i: (i, 0))
    widths = (2 * BRANCH_W, OFF_Q + N_HEADS * HEAD_DIM - OFF_POOL, 2 * KV_W, N_BRANCH * d)
    return pl.pallas_call(
        _in_proj_kernel,
        grid=(n // tm,),
        in_specs=[tok(d), _layer_row(g, layer), tok(LANES), tok(LANES),
                  pl.BlockSpec((None, d, d_in), lambda i: (layer, 0, 0), pipeline_mode=pl.Buffered(1))],
        out_specs=[tok(w) for w in widths],
        out_shape=[jax.ShapeDtypeStruct((n, w), BF16) for w in widths],
        compiler_params=_params("parallel"),
        name="in_proj",
    )(x, g, cos, sin, w_in)


def _mixer_kernel(sinks_ref, conv_ref, convh_ref, pq_ref, poolh_ref, kv_ref, kvh_ref, gate_ref, x_ref,
                  convw_ref, poolw_ref, pscale_ref, wbr_ref, wout_ref, out_ref, att_ref, *, tile, layer):
    seq_tile = pl.program_id(1)
    not_first = seq_tile > 0
    w = BRANCH_W

    v = conv_ref[:, w:].astype(F32)
    vh = convh_ref[:, w:].astype(F32)
    ext = jnp.concatenate([jnp.where(not_first, vh, 0.0), v], axis=0)
    cw = convw_ref[...]
    y = cw[2:3] * v + cw[1:2] * pltpu.roll(ext, 1, 0)[HALO:] + cw[0:1] * pltpu.roll(ext, 2, 0)[HALO:]
    y_conv = (conv_ref[:, :w].astype(F32) * y).astype(BF16)

    pu = pq_ref[:, :w].astype(F32)
    s = jnp.concatenate([jnp.where(not_first, poolh_ref[...].astype(F32), 0.0), pu], axis=0)
    pos1 = seq_tile * tile + lax.broadcasted_iota(jnp.int32, (tile, 1), 0) + 1
    mixed = []
    for gi, win in enumerate(POOL_WINDOWS):
        s = s[:, (POOL_GROUP_W if gi else 0):]
        s = s + pltpu.roll(s, win // 2, 0)
        inv_count = 1.0 / jnp.minimum(pos1, win).astype(F32)
        lo_c = gi * POOL_GROUP_W
        pooled = s[HALO:, :POOL_GROUP_W] * inv_count - pu[:, lo_c:lo_c + POOL_GROUP_W]
        mixed.append(_dot(pooled.astype(BF16), poolw_ref[gi]))
    y_pool = (jnp.concatenate(mixed, axis=1) * pscale_ref[...]).astype(BF16)

    d = x_ref.shape[1]
    gated = lambda r, br: gate_ref[:, r * d:(r + 1) * d].astype(F32) * _dot(br, wbr_ref[r])
    early = {0: y_conv, 1: y_pool}
    merged = []

    kvh = jnp.where(not_first, kvh_ref[...], jnp.zeros_like(kvh_ref[...]))
    kext = jnp.concatenate([kvh, kv_ref[...]], axis=0)
    qw = HEADS_PER_KV * HEAD_DIM
    rows = HEADS_PER_KV * WINDOW
    lo = lax.broadcasted_iota(jnp.int32, (2 * WINDOW, KV_W), 1) < HEAD_DIM
    lane_head = lax.broadcasted_iota(jnp.int32, (WINDOW, qw), 1) // HEAD_DIM
    qi = lax.broadcasted_iota(jnp.int32, (rows, 2 * WINDOW), 0) % WINDOW
    key = lax.broadcasted_iota(jnp.int32, (rows, 2 * WINDOW), 1)
    dist = qi + WINDOW - key
    in_band = (dist >= 0) & (dist < WINDOW)
    row_head = lax.broadcasted_iota(jnp.int32, (rows, 1), 0) // WINDOW
    for j in range(tile // WINDOW):
        band = kext[j * WINDOW:(j + 2) * WINDOW].astype(F32)
        halves = [(t, pltpu.roll(t, HEAD_DIM, 1)) for t in (band[:, :KV_W], band[:, KV_W:])]
        valid = in_band if j else in_band & ((key >= WINDOW) | not_first)
        for g in range(N_KV_HEADS):
            k_rep, v_rep = (
                jnp.concatenate([jnp.where(lo, r, t) if g else jnp.where(lo, t, r)] * (qw // KV_W), axis=1).astype(BF16)
                for t, r in halves)
            qg = pq_ref[j * WINDOW:(j + 1) * WINDOW, w + g * qw:w + (g + 1) * qw]
            q_heads = jnp.concatenate(
                [jnp.where(lane_head == hh, qg, jnp.zeros_like(qg)) for hh in range(HEADS_PER_KV)], axis=0)
            scores = lax.dot_general(q_heads, k_rep, (((1,), (1,)), ((), ())), preferred_element_type=F32)
            sink = jnp.zeros((rows, 1), F32)
            for hh in range(HEADS_PER_KV):
                sink = jnp.where(row_head == hh, sinks_ref[layer, g * HEADS_PER_KV + hh], sink)
            sh = jnp.where(valid, scores, NEG_INF)
            m = jnp.maximum(jnp.max(sh, axis=-1, keepdims=True), sink)
            pexp = jnp.exp(sh - m)
            denom = jnp.sum(pexp, axis=-1, keepdims=True) + jnp.exp(sink - m)
            res = _dot((pexp * (1.0 / denom)).astype(BF16), v_rep)
            out = res[:WINDOW]
            for hh in range(1, HEADS_PER_KV):
                out = jnp.where(lane_head == hh, res[hh * WINDOW:(hh + 1) * WINDOW], out)
            att_ref[j * WINDOW:(j + 1) * WINDOW, g * qw:(g + 1) * qw] = out.astype(BF16)
        if j in early:
            merged.append(gated(j, early[j]))

    merged = merged[0] + merged[1] + gated(2, att_ref[...])
    out_ref[...] = x_ref[...] + _dot(merged.astype(BF16), wout_ref[...])


def _mixer(x, conv, pq, kv, gates, sinks, conv_w, pool_w, pool_scale, w_branch, w_out, layer, batch, tile):
    n, d = x.shape
    n_t = n // batch // tile
    t_idx = lambda b, i: b * n_t + i
    tok = lambda width: pl.BlockSpec((tile, width), lambda b, i: (t_idx(b, i), 0))
    prev = lambda rows, width: pl.BlockSpec(
        (rows, width), lambda b, i: (jnp.maximum(t_idx(b, i) * (tile // rows) - 1, 0), 0))
    return pl.pallas_call(
        functools.partial(_mixer_kernel, tile=tile, layer=layer),
        grid=(batch, n_t),
        in_specs=[pl.BlockSpec(memory_space=pltpu.SMEM),
                  tok(conv.shape[1]), prev(HALO, conv.shape[1]),
                  tok(pq.shape[1]), prev(HALO, BRANCH_W),
                  tok(kv.shape[1]), prev(WINDOW, kv.shape[1]),
                  tok(gates.shape[1]), tok(d),
                  _layer_full(conv_w, layer), _layer_full(pool_w, layer), _layer_row(pool_scale, layer),
                  _layer_full(w_branch, layer), _layer_full(w_out, layer)],
        out_specs=tok(d),
        out_shape=jax.ShapeDtypeStruct((n, d), F32),
        scratch_shapes=[pltpu.VMEM((tile, N_HEADS * HEAD_DIM), BF16)],
        compiler_params=_params("parallel", "parallel"),
        name="mixer",
    )(sinks, conv, conv, pq, pq, kv, kv, gates, x, conv_w, pool_w, pool_scale, w_branch, w_out)


def _ffn_kernel(x_ref, g_ref, wg_ref, wu_ref, wd_ref, p_ref, gp_ref, wpg_ref, wpp_ref, gf_ref, out_ref, h_ref, acc_ref,
                *, final):
    j = pl.program_id(1)

    @pl.when(j == 0)
    def _():
        h_ref[...] = _rms(x_ref[...], g_ref[...]).astype(BF16)
        acc_ref[...] = jnp.zeros_like(acc_ref)

    h = h_ref[...]
    acts = []
    tf = wg_ref.shape[1]
    for c in range(0, tf, FFN_CHUNK):
        cols = slice(c, min(c + FFN_CHUNK, tf))
        hg = _dot(h, wg_ref[:, cols])
        acts.append((hg * _sigmoid(hg) * _dot(h, wu_ref[:, cols])).astype(BF16))
    acc_ref[...] += _dot(jnp.concatenate(acts, axis=1), wd_ref[...])

    @pl.when(j == pl.num_programs(1) - 1)
    def _():
        out_ref[...] = _ple_tail(x_ref[...] + acc_ref[...], p_ref, gp_ref, wpg_ref, wpp_ref, gf_ref, final)


def _ffn(x, g, w_gu, w_down, ple, layer, idx, final, tm, tf):
    n, d = x.shape
    n_f = w_down.shape[1] // tf
    mode = dict(pipeline_mode=pl.Buffered(1)) if n_f == 1 else {}
    return pl.pallas_call(
        functools.partial(_ffn_kernel, final=final),
        grid=(n // tm, n_f),
        in_specs=[pl.BlockSpec((tm, d), lambda i, j: (i, 0)),
                  _layer_row(g, layer),
                  pl.BlockSpec((None, d, tf), lambda i, j: (idx, 0, j), **mode),
                  pl.BlockSpec((None, d, tf), lambda i, j: (idx, 0, j + n_f), **mode),
                  pl.BlockSpec((None, tf, d), lambda i, j: (idx, j, 0), **mode)] + _ple_specs(ple, layer, tm),
        out_specs=pl.BlockSpec((tm, d), lambda i, j: (i, 0)),
        out_shape=jax.ShapeDtypeStruct((n, d), F32),
        scratch_shapes=[pltpu.VMEM((tm, d), BF16), pltpu.VMEM((tm, d), F32)],
        compiler_params=_params("parallel", "arbitrary"),
        name="ffn",
    )(x, g, w_gu, w_gu, w_down, *ple)


SEG_ALIGN = 8
R_LOCAL, R_WEIGHT = 2, 4
SEG_UNITS, SEG_LOCAL, SEG_BASE = 0, 8, 16


def _local_rows(tm, n_experts):
    return -(-(2 * tm + SEG_ALIGN * n_experts) // LANES) * LANES


def _router_kernel(x_ref, g_ref, wr_ref, route_ref, seg_ref, count_ref, base_ref, *, n_experts):
    @pl.when(pl.program_id(0) == 0)
    def _():
        base_ref[...] = jnp.zeros_like(base_ref)

    tm = x_ref.shape[0]
    h = _rms(x_ref[...], g_ref[...]).astype(BF16)
    lane_i = lax.broadcasted_iota(jnp.int32, (tm, LANES), 1)
    lane = lane_i.astype(F32)
    logits = jnp.where(lane_i < n_experts, _dot(h, wr_ref[...]), NEG_INF)

    def take_max(vals):
        top = jnp.max(vals, axis=-1, keepdims=True)
        idx = jnp.min(jnp.where(vals == top, lane, float(LANES)), axis=-1, keepdims=True)
        return top, idx

    v1, i1 = take_max(logits)
    v2, i2 = take_max(jnp.where(lane == i1, NEG_INF, logits))
    e2 = jnp.exp(v2 - v1)
    w1 = 1.0 / (1.0 + e2)
    hit1, hit2 = lane == i1, lane == i2
    onehot = jnp.where(hit1 | hit2, 1.0, 0.0)
    earlier = (lax.broadcasted_iota(jnp.int32, (tm, tm), 0) > lax.broadcasted_iota(jnp.int32, (tm, tm), 1))
    before = _dot(earlier.astype(BF16), onehot.astype(BF16))
    units = jnp.ceil(jnp.sum(onehot, axis=0, keepdims=True) * (1.0 / SEG_ALIGN))
    lower = (lax.broadcasted_iota(jnp.int32, (LANES, LANES), 0) < lax.broadcasted_iota(jnp.int32, (LANES, LANES), 1))
    local = SEG_ALIGN * _dot(jnp.broadcast_to(units, (SEG_ALIGN, LANES)).astype(BF16), lower.astype(BF16))[:1]
    row = before + local
    l1 = jnp.sum(jnp.where(hit1, row, 0.0), axis=-1, keepdims=True)
    l2 = jnp.sum(jnp.where(hit2, row, 0.0), axis=-1, keepdims=True)
    route = jnp.zeros((tm, LANES), F32)
    for k, val in enumerate((i1, i2, l1, l2, w1, e2 * w1)):
        route = jnp.where(lane_i == k, val, route)
    route_ref[...] = route
    roll_to = lambda a, k: pltpu.roll(a, k, 1)
    first = lane_i[:1] < n_experts
    seg = (jnp.where(first, units, 0.0) + roll_to(jnp.where(first, local, 0.0), SEG_LOCAL)
           + roll_to(jnp.where(first, base_ref[...], 0.0), SEG_BASE))
    seg_ref[...] = seg
    base_ref[...] += SEG_ALIGN * units
    count_ref[...] = base_ref[...]


def _router(x, g, w_router, layer, tm):
    n, d = x.shape
    n_experts = w_router.shape[1]
    assert n_experts <= SEG_LOCAL
    wr = jnp.pad(w_router, ((0, 0), (0, LANES - n_experts))).astype(BF16)
    return pl.pallas_call(
        functools.partial(_router_kernel, n_experts=n_experts),
        grid=(n // tm,),
        in_specs=[pl.BlockSpec((tm, d), lambda i: (i, 0)), _layer_row(g, layer),
                  pl.BlockSpec((d, LANES), lambda i: (0, 0))],
        out_specs=[pl.BlockSpec((tm, LANES), lambda i: (i, 0)),
                   pl.BlockSpec((None, 1, LANES), lambda i: (i, 0, 0)),
                   pl.BlockSpec((1, LANES), lambda i: (0, 0))],
        out_shape=[jax.ShapeDtypeStruct((n, LANES), F32), jax.ShapeDtypeStruct((n // tm, 1, LANES), F32),
                   jax.ShapeDtypeStruct((1, LANES), F32)],
        scratch_shapes=[pltpu.VMEM((1, LANES), F32)],
        compiler_params=_params("arbitrary"),
        name="router",
    )(x, g, wr)


def _for_each_chunk(seg_ref, n_experts, max_units, fn):
    for e in range(n_experts):
        units = seg_ref[0, 0, SEG_UNITS + e]
        local = seg_ref[0, 0, SEG_LOCAL + e]
        grouped = seg_ref[0, 0, SEG_BASE + e]
        for b in range(max_units.bit_length()):
            skip = (units >> (b + 1) << (b + 1)) * SEG_ALIGN

            @pl.when((units >> b) & 1 == 1)
            def _():
                fn(pl.multiple_of(local + skip, SEG_ALIGN), pl.multiple_of(grouped + skip, SEG_ALIGN), SEG_ALIGN << b)


def _dispatch_kernel(pad_ref, seg_ref, prev_seg_ref, x_ref, g_ref, route_ref, xs_ref, sorted_ref, zero_ref, sem,
                     *, n_experts, n_tokens):
    step = pl.program_id(0)
    tm = x_ref.shape[0]
    tmg = zero_ref.shape[0]
    max_tail = (xs_ref.shape[0] - 2 * n_tokens) // tmg
    fill = lambda row: pltpu.make_async_copy(zero_ref, xs_ref.at[pl.ds(pl.multiple_of(row, SEG_ALIGN), tmg)], sem.at[0])

    @pl.when(step == 0)
    def _():
        zero_ref[...] = jnp.zeros_like(zero_ref)
        for e in range(n_experts):
            fill(pad_ref[e]).start()
        for e in range(n_experts):
            fill(pad_ref[e]).wait()
        for t in range(max_tail):
            row = pad_ref[n_experts] + t * tmg

            @pl.when(row < xs_ref.shape[0])
            def _():
                fill(row).start()
                fill(row).wait()

    slot = step % 2
    n_local = sorted_ref.shape[1]
    h = _rms(x_ref[...], g_ref[...]).astype(BF16)
    owner = route_ref[...].T
    row = lax.broadcasted_iota(jnp.int32, (n_local, tm), 0).astype(F32)
    pick = (row == owner[R_LOCAL:R_LOCAL + 1]) | (row == owner[R_LOCAL + 1:R_LOCAL + 2])
    sorted_ref[slot] = _dot(jnp.where(pick, 1.0, 0.0).astype(BF16), h)

    def segment_copy(buf_slot):
        return lambda local, grouped, rows: pltpu.make_async_copy(
            sorted_ref.at[buf_slot, pl.ds(local, rows)], xs_ref.at[pl.ds(grouped, rows)], sem.at[buf_slot])

    _for_each_chunk(seg_ref, n_experts, tm // SEG_ALIGN, lambda *a: segment_copy(slot)(*a).start())

    @pl.when(step > 0)
    def _():
        _for_each_chunk(prev_seg_ref, n_experts, tm // SEG_ALIGN, lambda *a: segment_copy(1 - slot)(*a).wait())

    @pl.when(step == pl.num_programs(0) - 1)
    def _():
        _for_each_chunk(seg_ref, n_experts, tm // SEG_ALIGN, lambda *a: segment_copy(slot)(*a).wait())


def _dispatch(x, g, layer, route, seg, pad_start, rows, tm, tmg):
    n, d = x.shape
    n_experts = pad_start.shape[0] - 1
    seg_spec = lambda shift: pl.BlockSpec((1, 1, seg.shape[2]), lambda i, pad: (jnp.maximum(i - shift, 0), 0, 0),
                                          memory_space=pltpu.SMEM)
    return pl.pallas_call(
        functools.partial(_dispatch_kernel, n_experts=n_experts, n_tokens=n),
        grid_spec=pltpu.PrefetchScalarGridSpec(
            num_scalar_prefetch=1,
            grid=(n // tm,),
            in_specs=[seg_spec(0), seg_spec(1),
                      pl.BlockSpec((tm, d), lambda i, pad: (i, 0)),
                      _layer_row(g, layer),
                      pl.BlockSpec((tm, LANES), lambda i, pad: (i, 0))],
            out_specs=pl.BlockSpec(memory_space=pl.ANY),
            scratch_shapes=[pltpu.VMEM((2, _local_rows(tm, n_experts), d), F32), pltpu.VMEM((tmg, d), F32),
                            pltpu.SemaphoreType.DMA((2,))],
        ),
        out_shape=jax.ShapeDtypeStruct((rows + tmg, d), F32),
        compiler_params=_params("arbitrary"),
        name="dispatch",
    )(pad_start, seg, seg, x, g, route)


def _expert_kernel(te_ref, used_ref, xs_ref, wg_ref, wu_ref, wd_ref, y_ref):
    @pl.when(pl.program_id(0) < used_ref[0])
    def _():
        h = xs_ref[...].astype(BF16)
        hg = _dot(h, wg_ref[0])
        act = (hg * _sigmoid(hg) * _dot(h, wu_ref[0])).astype(BF16)
        y_ref[...] = _dot(act, wd_ref[0])

    @pl.when(pl.program_id(0) >= used_ref[0])
    def _():
        y_ref[...] = jnp.zeros_like(y_ref)


def _experts(xs, tile_expert, n_used, w_gu, w_down, idx, rows, tmg):
    d = xs.shape[1]
    d_ff = w_down.shape[2]
    live = lambda t, te, used: jnp.minimum(t, used[0] - 1)
    return pl.pallas_call(
        _expert_kernel,
        grid_spec=pltpu.PrefetchScalarGridSpec(
            num_scalar_prefetch=2,
            grid=(rows // tmg,),
            in_specs=[pl.BlockSpec((tmg, d), lambda t, te, used: (live(t, te, used), 0)),
                      pl.BlockSpec((None, 1, d, d_ff), lambda t, te, used: (idx, te[live(t, te, used)], 0, 0)),
                      pl.BlockSpec((None, 1, d, d_ff), lambda t, te, used: (idx, te[live(t, te, used)], 0, 1)),
                      pl.BlockSpec((None, 1, d_ff, d), lambda t, te, used: (idx, te[live(t, te, used)], 0, 0))],
            out_specs=pl.BlockSpec((tmg, d), lambda t, te, used: (t, 0)),
        ),
        out_shape=jax.ShapeDtypeStruct((rows, d), F32),
        compiler_params=_params("arbitrary"),
        name="experts",
    )(tile_expert, n_used, xs, w_gu, w_gu, w_down)


def _combine_kernel(seg_ref, next_seg_ref, x_ref, route_ref, y_ref, p_ref, gp_ref, wpg_ref, wpp_ref, gf_ref,
                    out_ref, buf_ref, sem, *, n_experts, final):
    step = pl.program_id(0)
    tm = x_ref.shape[0]
    slot = step % 2
    n_local = buf_ref.shape[1]

    def segment_copy(buf_slot):
        return lambda local, grouped, rows: pltpu.make_async_copy(
            y_ref.at[pl.ds(grouped, rows)], buf_ref.at[buf_slot, pl.ds(local, rows)], sem.at[buf_slot])

    @pl.when(step == 0)
    def _():
        buf_ref[...] = jnp.zeros_like(buf_ref)
        _for_each_chunk(seg_ref, n_experts, tm // SEG_ALIGN, lambda *a: segment_copy(slot)(*a).start())

    @pl.when(step < pl.num_programs(0) - 1)
    def _():
        _for_each_chunk(next_seg_ref, n_experts, tm // SEG_ALIGN, lambda *a: segment_copy(1 - slot)(*a).start())

    _for_each_chunk(seg_ref, n_experts, tm // SEG_ALIGN, lambda *a: segment_copy(slot)(*a).wait())

    route = route_ref[...]
    y = buf_ref[slot].astype(BF16)
    col = lax.broadcasted_iota(jnp.int32, (tm, n_local), 1).astype(F32)
    weigh = jnp.zeros((tm, n_local), F32)
    for k in range(2):
        weigh = jnp.where(col == route[:, R_LOCAL + k:R_LOCAL + k + 1], route[:, R_WEIGHT + k:R_WEIGHT + k + 1], weigh)
    x2 = x_ref[...] + _dot(weigh.astype(BF16), y)
    out_ref[...] = _ple_tail(x2, p_ref, gp_ref, wpg_ref, wpp_ref, gf_ref, final)


def _combine(x, route, seg, y, ple, layer, final, tm):
    n, d = x.shape
    n_experts = SEG_LOCAL - SEG_UNITS
    last = n // tm - 1
    seg_spec = lambda shift: pl.BlockSpec((1, 1, seg.shape[2]), lambda i: (jnp.minimum(i + shift, last), 0, 0),
                                          memory_space=pltpu.SMEM)
    return pl.pallas_call(
        functools.partial(_combine_kernel, n_experts=n_experts, final=final),
        grid=(n // tm,),
        in_specs=[seg_spec(0), seg_spec(1),
                  pl.BlockSpec((tm, d), lambda i: (i, 0)),
                  pl.BlockSpec((tm, LANES), lambda i: (i, 0)),
                  pl.BlockSpec(memory_space=pl.ANY)] + _ple_specs(ple, layer, tm),
        out_specs=pl.BlockSpec((tm, d), lambda i: (i, 0)),
        out_shape=jax.ShapeDtypeStruct((n, d), F32),
        scratch_shapes=[pltpu.VMEM((2, _local_rows(tm, n_experts), d), F32), pltpu.SemaphoreType.DMA((2,))],
        compiler_params=_params("arbitrary"),
        name="combine",
    )(seg, seg, x, route, y, *ple)


def _moe(x, g, layer, w_router, w_gu, w_down, idx, ple, final, tm, tmg):
    n = x.shape[0]
    n_experts = w_router.shape[1]
    route, seg, count = _router(x, g, w_router, layer, tm)
    counts = count[0, :n_experts].astype(jnp.int32)
    padded = (counts + tmg - 1) // tmg * tmg
    ends = jnp.cumsum(padded)
    starts = ends - padded
    seg = seg.astype(jnp.int32)[:, :, :4 * SEG_LOCAL]
    seg = seg.at[:, 0, SEG_BASE:SEG_BASE + n_experts].add(starts)
    rows = -(-(2 * n + (n // tm) * n_experts * SEG_ALIGN) // tmg) * tmg + n_experts * tmg
    n_used = (ends[-1:] // tmg).astype(jnp.int32)
    tile_row = jnp.arange(rows // tmg, dtype=jnp.int32) * tmg
    tile_expert = jnp.minimum(jnp.sum(tile_row[:, None] >= ends[None, :], axis=-1), n_experts - 1).astype(jnp.int32)
    pad_start = jnp.concatenate([starts + counts, ends[-1:]]).astype(jnp.int32)

    xs = _dispatch(x, g, layer, route, seg, pad_start, rows, tm, tmg)
    y = _experts(xs, tile_expert, n_used, w_gu, w_down, idx, rows, tmg)
    return _combine(x, route, seg, y, ple, layer, final, tm)


def _tiles(n, seq):
    return min(512, n), min(1024, n), min(512, seq), min(512, n), 512


def _ff_tile(d, d_ff):
    if 3 * d * d_ff * 2 <= V7X_VMEM_LIMIT_BYTES // 2:
        return d_ff
    half = d_ff // 2
    if d_ff % 2 == 0 and half % V7X_MXU_WIDTH == 0:
        return half
    for tf in (512, 256, 128):
        if d_ff % tf == 0:
            return tf
    raise ValueError(f"unsupported hidden width {d_ff}")


def kernel(x, p, positions, norm_mix_g, w_in, conv_w, pool_w, pool_scale, attn_sinks, w_branch, w_out, norm_ffn_g, ffn_w_gu, ffn_w_down, moe_router, moe_w_gu, moe_w_down, norm_ple_g, ple_w_proj, ple_w_gate, final_norm_g):
    batch, seq, d = x.shape
    depth = w_in.shape[0]
    n = batch * seq
    tm_in, tm, tile, tm_row, tmg = _tiles(n, seq)
    assert seq % tile == 0 and tile % WINDOW == 0 and n % tm == 0 and n % tm_in == 0 and n % tm_row == 0
    stack_rows = lambda a: a.reshape(a.shape[0], 1, a.shape[1])

    w_in, pool_w, w_branch, w_out = (a.astype(BF16) for a in (w_in, pool_w, w_branch, w_out))
    ffn_w_gu, ffn_w_down, moe_w_gu, moe_w_down = (a.astype(BF16) for a in (ffn_w_gu, ffn_w_down, moe_w_gu, moe_w_down))
    ple_w_gate, ple_w_proj = ple_w_gate.astype(BF16), ple_w_proj.astype(BF16)
    g_mix, g_ffn, g_ple = stack_rows(norm_mix_g), stack_rows(norm_ffn_g), stack_rows(norm_ple_g)
    pool_scale = stack_rows(pool_scale)
    ple = (p.reshape(depth, n, -1), g_ple, ple_w_gate, ple_w_proj, final_norm_g.reshape(1, d))

    cos, sin = _rope_tables(positions, tm)
    xs = x.reshape(n, d)
    for i in range(depth):
        conv, pq, kv, gates = _in_proj(xs, g_mix, cos, sin, w_in, i, tm)
        xs = _mixer(xs, conv, pq, kv, gates, attn_sinks, conv_w, pool_w, pool_scale, w_branch, w_out, i, batch, tile)
        final = i == depth - 1
        if i % 2 == 0:
            xs = _ffn(xs, g_ffn, ffn_w_gu, ffn_w_down, ple, i, i // 2, final, tm_in, _ff_tile(d, ffn_w_down.shape[1]))
        else:
            xs = _moe(xs, g_ffn, i, moe_router[i // 2], moe_w_gu, moe_w_down, i // 2, ple, final, tm_row, tmg)
    return xs.reshape(batch, seq, d)
```
